```python
import math
import jax, jax.numpy as jnp
from jax import lax
import numpy as np

D_MODEL = 1024
BATCH = 16
SEQ = 4096
DEPTH = 4
DEC_BATCH = 32
DEC_SEQ = 16
PAST_LEN = 2048

CHUNK = 64
Q_BLOCK = 128
A_HEADS = 8
A_QK = 64
A_V = 2 * A_QK
A_W = A_HEADS * A_V
R_W = D_MODEL
R_BLOCKS = 8
R_BS = R_W // R_BLOCKS
CONV_W = 4
RG_C = 8.0
M_TOK = 256
M_HEADS = 4
M_HD = 256
M_W = M_HEADS * M_HD
N_BRANCH = 3
BR_W = 1024
IN_SIZES = (2 * A_HEADS * A_QK, 2 * A_HEADS * A_QK, A_W, A_W, R_W, R_W, M_W, M_W, N_BRANCH * D_MODEL)
IN_COLS = 11264
ALPHA = (2 * DEPTH) ** 0.25
BETA = (8 * DEPTH) ** -0.25
EPS = 1e-5
NEG_INF = -1e30

kernel_name = "hybrid_diffattn_rglru_memx_stream_step"


def _layer_norm(x, g, b):
    xf = x.astype(jnp.float32)
    mu = jnp.mean(xf, axis=-1, keepdims=True)
    xc = xf - mu
    var = jnp.mean(xc * xc, axis=-1, keepdims=True)
    return (xc * lax.rsqrt(var + EPS) * g + b).astype(x.dtype)


def _alibi_slopes():
    return jnp.exp2(-8.0 * jnp.arange(1, A_HEADS + 1, dtype=jnp.float32) / A_HEADS)


def _diff_attn(q, k, v, pos_q, pos_k, lam, lam_scale, subln_g):
    scale = A_QK ** -0.5
    dist = jnp.abs(pos_q[:, None] - pos_k[None, :]).astype(jnp.float32)
    visible = (pos_k[None, :] // CHUNK) <= (pos_q[:, None] // CHUNK)
    bias = jnp.where(visible[None], -_alibi_slopes()[:, None, None] * dist[None], NEG_INF)
    s1 = jnp.einsum('bqhd,bkhd->bhqk', q[..., :A_QK], k[..., :A_QK]).astype(jnp.float32) * scale + bias
    s2 = jnp.einsum('bqhd,bkhd->bhqk', q[..., A_QK:], k[..., A_QK:]).astype(jnp.float32) * scale + bias
    w = jax.nn.softmax(s1, axis=-1) - lam * jax.nn.softmax(s2, axis=-1)
    o = jnp.einsum('bhqk,bkhd->bqhd', w.astype(v.dtype), v).astype(jnp.float32)
    o = o * lax.rsqrt(jnp.mean(o * o, axis=-1, keepdims=True) + EPS) * subln_g * lam_scale
    return o.astype(v.dtype)


def _prompt_diff_attn(q, k, v, lam, lam_scale, subln_g):
    B, S = q.shape[0], q.shape[1]
    nb = S // Q_BLOCK
    pos_k = jnp.arange(S)
    qb = q.reshape(B, nb, Q_BLOCK, A_HEADS, 2 * A_QK).swapaxes(0, 1)

    def block(args):
        i, q_blk = args
        return _diff_attn(q_blk, k, v, i * Q_BLOCK + jnp.arange(Q_BLOCK), pos_k, lam, lam_scale, subln_g)

    o = lax.map(block, (jnp.arange(nb), qb))
    return o.swapaxes(0, 1).reshape(B, S, A_HEADS, A_V)


def _rglru(x_c, h0, rg_wa, rg_ba, rg_wx, rg_bx, rg_lambda):
    B, T, _ = x_c.shape
    xf = x_c.astype(jnp.float32)
    xb = xf.reshape(B, T, R_BLOCKS, R_BS)
    r = jax.nn.sigmoid(jnp.einsum('btnd,nde->btne', xb, rg_wa.astype(jnp.float32)).reshape(B, T, R_W) + rg_ba)
    i = jax.nn.sigmoid(jnp.einsum('btnd,nde->btne', xb, rg_wx.astype(jnp.float32)).reshape(B, T, R_W) + rg_bx)
    log_a = -RG_C * r * jax.nn.softplus(-rg_lambda.astype(jnp.float32))
    a = jnp.exp(log_a)
    b = jnp.sqrt(-jnp.expm1(2.0 * log_a)) * (i * xf)
    b = b.at[:, 0].add(a[:, 0] * h0.astype(jnp.float32))

    def comb(e1, e2):
        a1, b1 = e1
        a2, b2 = e2
        return a1 * a2, a2 * b1 + b2

    _, h = lax.associative_scan(comb, (a, b), axis=1)
    return h, h[:, -1]


def _layer(x, mem_k, mem_v, k_past, v_past, h0, conv_buf, lam, lam_scale,
           w_in, subln_g, conv_w, conv_b, rg_wa, rg_ba, rg_wx, rg_bx, rg_lambda,
           w_branch, w_o, ln_g, ln_b):
    B, T, _ = x.shape
    splits = np.cumsum(IN_SIZES)[:-1].tolist()
    q, k, v, g_a, x_r, g_b, q_m, g_c, g_m = jnp.split(jnp.einsum('btd,dc->btc', x, w_in), splits, axis=-1)
    q = q.reshape(B, T, A_HEADS, 2 * A_QK)
    k = k.reshape(B, T, A_HEADS, 2 * A_QK)
    v = v.reshape(B, T, A_HEADS, A_V)
    if k_past is None:
        o_a = _prompt_diff_attn(q, k, v, lam, lam_scale, subln_g)
    else:
        past = k_past.shape[1]
        kk = jnp.concatenate([k_past.astype(k.dtype), k], axis=1)
        vv = jnp.concatenate([v_past.astype(v.dtype), v], axis=1)
        o_a = _diff_attn(q, kk, vv, past + jnp.arange(T), jnp.arange(past + T), lam, lam_scale, subln_g)
    o_a = o_a.reshape(B, T, A_W) * jax.nn.silu(g_a)
    xpad = jnp.concatenate([conv_buf.astype(x.dtype), x_r], axis=1)
    x_c = conv_b + sum(xpad[:, j:j + T] * conv_w[j] for j in range(CONV_W))
    h, h_last = _rglru(x_c, h0, rg_wa, rg_ba, rg_wx, rg_bx, rg_lambda)
    o_b = h.astype(x.dtype) * jax.nn.silu(g_b)
    q_m = q_m.reshape(B, T, M_HEADS, M_HD)
    s_m = jnp.einsum('bqhd,bmhd->bhqm', q_m, mem_k.astype(x.dtype)).astype(jnp.float32) * (M_HD ** -0.5)
    p_m = jax.nn.softmax(s_m, axis=-1).astype(x.dtype)
    o_c = jnp.einsum('bhqm,bmhd->bqhd', p_m, mem_v.astype(x.dtype)).reshape(B, T, M_W) * jax.nn.silu(g_c)
    o = jnp.stack([o_a, o_b, o_c], axis=2)
    p = jnp.einsum('btnw,nwd->btnd', o, w_branch)
    gate = jax.nn.sigmoid(g_m.reshape(B, T, N_BRANCH, D_MODEL))
    out = jnp.einsum('btd,de->bte', jnp.sum(gate * p, axis=2), w_o)
    y = _layer_norm(ALPHA * x + out, ln_g, ln_b)
    return y, k, v, h_last, xpad[:, -(CONV_W - 1):]


def setup_inputs(seed: int = 0) -> dict:
    key = jax.random.key(seed)
    ks = jax.random.split(key, 32)
    f32 = jnp.float32

    def nrm(k, shape, s):
        return jax.random.normal(k, shape, f32) * s

    u = jax.random.uniform(ks[20], (DEPTH, R_W), f32, 0.9, 0.999)
    s = u ** (1.0 / RG_C)
    rg_lambda = jnp.log(s) - jnp.log1p(-s)
    return {
        "x_prompt": nrm(ks[0], (BATCH, SEQ, D_MODEL), 1.0),
        "x_sample": nrm(ks[1], (DEC_BATCH, DEC_SEQ, D_MODEL), 1.0),
        "cache_k": nrm(ks[2], (DEPTH, DEC_BATCH, PAST_LEN, A_HEADS, 2 * A_QK), 1.0),
        "cache_v": nrm(ks[3], (DEPTH, DEC_BATCH, PAST_LEN, A_HEADS, A_V), 1.0),
        "cache_mem_k": nrm(ks[4], (DEPTH, DEC_BATCH, M_TOK, M_HEADS, M_HD), 1.0),
        "cache_mem_v": nrm(ks[5], (DEPTH, DEC_BATCH, M_TOK, M_HEADS, M_HD), 1.0),
        "state_rnn_h": nrm(ks[6], (DEPTH, DEC_BATCH, R_W), 0.5),
        "state_conv": nrm(ks[7], (DEPTH, DEC_BATCH, CONV_W - 1, R_W), 1.0),
        "mem_prompt": nrm(ks[8], (BATCH, M_TOK, D_MODEL), 1.0),
        "ln_in_g": 1.0 + nrm(ks[9], (D_MODEL,), 0.02),
        "ln_in_b": nrm(ks[10], (D_MODEL,), 0.01),
        "w_in": nrm(ks[11], (DEPTH, D_MODEL, IN_COLS), D_MODEL ** -0.5),
        "lambda_q1": nrm(ks[12], (DEPTH, A_QK), 0.1),
        "lambda_k1": nrm(ks[13], (DEPTH, A_QK), 0.1),
        "lambda_q2": nrm(ks[14], (DEPTH, A_QK), 0.1),
        "lambda_k2": nrm(ks[15], (DEPTH, A_QK), 0.1),
        "subln_g": 1.0 + nrm(ks[16], (DEPTH, A_V), 0.02),
        "conv_w": nrm(ks[17], (DEPTH, CONV_W, R_W), CONV_W ** -0.5),
        "conv_b": nrm(ks[18], (DEPTH, R_W), 0.01),
        "rg_wa": nrm(ks[19], (DEPTH, R_BLOCKS, R_BS, R_BS), R_BS ** -0.5),
        "rg_ba": nrm(ks[21], (DEPTH, R_W), 0.01),
        "rg_wx": nrm(ks[22], (DEPTH, R_BLOCKS, R_BS, R_BS), R_BS ** -0.5),
        "rg_bx": nrm(ks[23], (DEPTH, R_W), 0.01),
        "rg_lambda": rg_lambda,
        "w_mem_kv": nrm(ks[24], (DEPTH, D_MODEL, 2 * M_W), D_MODEL ** -0.5),
        "w_branch": nrm(ks[25], (DEPTH, N_BRANCH, BR_W, D_MODEL), BETA * BR_W ** -0.5),
        "w_o": nrm(ks[26], (DEPTH, D_MODEL, D_MODEL), BETA * D_MODEL ** -0.5),
        "ln_g": 1.0 + nrm(ks[27], (DEPTH, D_MODEL), 0.02),
        "ln_b": nrm(ks[28], (DEPTH, D_MODEL), 0.01),
    }


def reference(x_prompt, x_sample, cache_k, cache_v, cache_mem_k, cache_mem_v, state_rnn_h, state_conv,
              mem_prompt, ln_in_g, ln_in_b, w_in, lambda_q1, lambda_k1, lambda_q2, lambda_k2, subln_g,
              conv_w, conv_b, rg_wa, rg_ba, rg_wx, rg_bx, rg_lambda, w_mem_kv, w_branch, w_o, ln_g, ln_b):
    xp = _layer_norm(x_prompt, ln_in_g, ln_in_b)
    xs = _layer_norm(x_sample, ln_in_g, ln_in_b)
    Bp, Bs = x_prompt.shape[0], x_sample.shape[0]
    pk, pv, pmk, pmv, ph, pc = [], [], [], [], [], []
    sk, sv, sh, sc = [], [], [], []
    for l in range(DEPTH):
        lam_init = 0.8 - 0.6 * math.exp(-0.3 * l)
        lam = (jnp.exp(jnp.sum(lambda_q1[l].astype(jnp.float32) * lambda_k1[l].astype(jnp.float32)))
               - jnp.exp(jnp.sum(lambda_q2[l].astype(jnp.float32) * lambda_k2[l].astype(jnp.float32)))
               + lam_init)
        lw = (w_in[l], subln_g[l], conv_w[l], conv_b[l], rg_wa[l], rg_ba[l], rg_wx[l], rg_bx[l],
              rg_lambda[l], w_branch[l], w_o[l], ln_g[l], ln_b[l])
        mkv = jnp.einsum('bmd,dc->bmc', mem_prompt, w_mem_kv[l])
        mk = mkv[..., :M_W].reshape(Bp, M_TOK, M_HEADS, M_HD)
        mv = mkv[..., M_W:].reshape(Bp, M_TOK, M_HEADS, M_HD)
        h0 = jnp.zeros((Bp, R_W), jnp.float32)
        buf0 = jnp.zeros((Bp, CONV_W - 1, R_W), xp.dtype)
        xp, k_new, v_new, h_new, buf_new = _layer(xp, mk, mv, None, None, h0, buf0, lam, 1.0 - lam_init, *lw)
        pk.append(k_new); pv.append(v_new); pmk.append(mk); pmv.append(mv); ph.append(h_new); pc.append(buf_new)
        xs, k_new, v_new, h_new, buf_new = _layer(xs, cache_mem_k[l], cache_mem_v[l], cache_k[l], cache_v[l],
                                                  state_rnn_h[l], state_conv[l], lam, 1.0 - lam_init, *lw)
        sk.append(k_new); sv.append(v_new); sh.append(h_new); sc.append(buf_new)
    return (xp, xs,
            jnp.stack(pk), jnp.stack(pv), jnp.stack(pmk), jnp.stack(pmv), jnp.stack(ph), jnp.stack(pc),
            jnp.stack(sk), jnp.stack(sv), jnp.stack(sh), jnp.stack(sc))
```

```python
import functools
import math

import jax
import jax.numpy as jnp
from jax import lax
from jax.experimental import pallas as pl
from jax.experimental.pallas import tpu as pltpu

F32 = jnp.float32
BF16 = jnp.bfloat16

D_MODEL = 1024
CHUNK = 64
A_HEADS = 8
A_QK = 64
A_V = 2 * A_QK
R_W = D_MODEL
R_BLOCKS = 8
R_BS = R_W // R_BLOCKS
CONV_W = 4
RG_C = 8.0
M_HEADS = 4
M_HD = 256
N_BRANCH = 3
IN_COLS = 11 * D_MODEL
EPS = 1e-5
NEG_INF = -1e30
COL_Q, COL_K, COL_V, COL_GA, COL_XR, COL_GB, COL_QM, COL_GC, COL_GM = 0, 1, 2, 3, 4, 5, 6, 7, 8
HEAD_BLOCKS = D_MODEL // A_V

VMEM_LIMIT = 56 * 1024 * 1024


def _params(*sem):
    return pltpu.CompilerParams(dimension_semantics=sem, vmem_limit_bytes=VMEM_LIMIT)


def _nt_dot(a, b):
    return lax.dot_general(a, b, (((1,), (1,)), ((), ())), preferred_element_type=F32)


def _silu(x):
    return x * jax.nn.sigmoid(x)


def _layer_norm_rows(x, g, b):
    mu = jnp.mean(x, axis=-1, keepdims=True)
    xc = x - mu
    var = jnp.mean(xc * xc, axis=-1, keepdims=True)
    return xc * lax.rsqrt(var + EPS) * g + b


def _ln_kernel(x_ref, g_ref, b_ref, o_ref):
    o_ref[...] = _layer_norm_rows(x_ref[...], g_ref[...], b_ref[...])


def _input_ln(x, g, b):
    n = x.shape[0]
    tm = min(n, 1024)
    return pl.pallas_call(
        _ln_kernel,
        grid=(n // tm,),
        in_specs=[pl.BlockSpec((tm, D_MODEL), lambda i: (i, 0)),
                  pl.BlockSpec((1, D_MODEL), lambda i: (0, 0)),
                  pl.BlockSpec((1, D_MODEL), lambda i: (0, 0))],
        out_specs=pl.BlockSpec((tm, D_MODEL), lambda i: (i, 0)),
        out_shape=jax.ShapeDtypeStruct((n, D_MODEL), F32),
        compiler_params=_params("parallel"),
        name="input_ln",
    )(x, g.reshape(1, D_MODEL), b.reshape(1, D_MODEL))


def _proj_kernel(x_ref, w_ref, o_ref, xb_ref):
    @pl.when(pl.program_id(1) == 0)
    def _():
        xb_ref[...] = x_ref[...].astype(BF16)

    o_ref[...] = jnp.dot(xb_ref[...], w_ref[...], preferred_element_type=F32)


def _project(x, w_all, layer):
    n = x.shape[0]
    cols = w_all.shape[2]
    tm = min(n, 512)
    tn = D_MODEL
    return pl.pallas_call(
        _proj_kernel,
        grid=(n // tm, cols // tn),
        in_specs=[pl.BlockSpec((tm, D_MODEL), lambda i, j: (i, 0)),
                  pl.BlockSpec((None, D_MODEL, tn), lambda i, j: (layer, 0, j))],
        out_specs=pl.BlockSpec((tm, tn), lambda i, j: (i, j)),
        out_shape=jax.ShapeDtypeStruct((n, cols), F32),
        scratch_shapes=[pltpu.VMEM((tm, D_MODEL), BF16)],
        compiler_params=_params("parallel", "arbitrary"),
        name="projection",
    )(x, w_all)


def _masked_halves(q):
    lane = lax.broadcasted_iota(jnp.int32, q.shape, 1)
    qs = q * (A_QK ** -0.5)
    return (jnp.where(lane < A_QK, qs, 0.0).astype(BF16),
            jnp.where(lane >= A_QK, qs, 0.0).astype(BF16))


def _subln_gate(o, sg, lam_scale, g):
    o = o * lax.rsqrt(jnp.mean(o * o, axis=-1, keepdims=True) + EPS) * sg * lam_scale
    return o * _silu(g)


def _attn_prompt_kernel(scal_ref, q_ref, k_ref, v_ref, g_ref, sg_ref, o_ref,
                        kb_ref, vb_ref, m1_ref, l1_ref, a1_ref, m2_ref, l2_ref, a2_ref, *, tile):
    h = pl.program_id(1)
    i = pl.program_id(2)
    lam = scal_ref[0]
    lam_scale = scal_ref[1]
    slope = scal_ref[2 + h]

    @pl.when(i == 0)
    def _():
        kb_ref[...] = k_ref[...].astype(BF16)
        vb_ref[...] = v_ref[...].astype(BF16)

    q_lo, q_hi = _masked_halves(q_ref[...])
    row = lax.broadcasted_iota(jnp.int32, (tile, tile), 0)
    col = lax.broadcasted_iota(jnp.int32, (tile, tile), 1)
    rc = (row - col).astype(F32)

    for m_ref, l_ref, a_ref in ((m1_ref, l1_ref, a1_ref), (m2_ref, l2_ref, a2_ref)):
        m_ref[...] = jnp.full(m_ref.shape, NEG_INF, F32)
        l_ref[...] = jnp.zeros(l_ref.shape, F32)
        a_ref[...] = jnp.zeros(a_ref.shape, F32)

    def step(j, diagonal):
        off = pl.multiple_of(j * tile, tile)
        kt = kb_ref[pl.ds(off, tile), :]
        vt = vb_ref[pl.ds(off, tile), :]
        if diagonal:
            visible = (col // CHUNK) <= (row // CHUNK)
            bias = jnp.where(visible, -slope * jnp.abs(rc), NEG_INF)
        else:
            bias = -slope * (rc + ((i - j) * tile).astype(F32))
        for qq, m_ref, l_ref, a_ref in ((q_lo, m1_ref, l1_ref, a1_ref), (q_hi, m2_ref, l2_ref, a2_ref)):
            s = _nt_dot(qq, kt) + bias
            m_prev = m_ref[...]
            m_new = jnp.maximum(m_prev, jnp.max(s, axis=-1, keepdims=True))
            alpha = jnp.exp(m_prev - m_new)
            p = jnp.exp(s - m_new)
            l_ref[...] = alpha * l_ref[...] + jnp.sum(p, axis=-1, keepdims=True)
            a_ref[...] = alpha * a_ref[...] + jnp.dot(p.astype(BF16), vt, preferred_element_type=F32)
            m_ref[...] = m_new

    def body(j, carry):
        step(j, False)
        return carry

    lax.fori_loop(0, i, body, 0)
    step(i, True)

    o = a1_ref[...] / l1_ref[...] - lam * (a2_ref[...] / l2_ref[...])
    o_ref[...] = _subln_gate(o, sg_ref[...], lam_scale, g_ref[...])


def _attn_prompt(proj, scal, subln_g, batch, seq):
    tile = min(seq, 256)
    assert seq % tile == 0 and tile % CHUNK == 0
    kernel = functools.partial(_attn_prompt_kernel, tile=tile)
    hb = HEAD_BLOCKS
    return pl.pallas_call(
        kernel,
        grid=(batch, A_HEADS, seq // tile),
        in_specs=[pl.BlockSpec(memory_space=pltpu.SMEM),
                  pl.BlockSpec((None, tile, A_V), lambda b, h, i: (b, i, COL_Q * hb + h)),
                  pl.BlockSpec((None, seq, A_V), lambda b, h, i: (b, 0, COL_K * hb + h)),
                  pl.BlockSpec((None, seq, A_V), lambda b, h, i: (b, 0, COL_V * hb + h)),
                  pl.BlockSpec((None, tile, A_V), lambda b, h, i: (b, i, COL_GA * hb + h)),
                  pl.BlockSpec((1, A_V), lambda b, h, i: (0, 0))],
        out_specs=pl.BlockSpec((None, tile, A_V), lambda b, h, i: (b, i, h)),
        out_shape=jax.ShapeDtypeStruct((batch, seq, D_MODEL), F32),
        scratch_shapes=[pltpu.VMEM((seq, A_V), BF16), pltpu.VMEM((seq, A_V), BF16),
                        pltpu.VMEM((tile, 1), F32), pltpu.VMEM((tile, 1), F32), pltpu.VMEM((tile, A_V), F32),
                        pltpu.VMEM((tile, 1), F32), pltpu.VMEM((tile, 1), F32), pltpu.VMEM((tile, A_V), F32)],
        compiler_params=_params("parallel", "parallel", "arbitrary"),
        name="attn_prompt",
    )(scal, proj, proj, proj, proj, subln_g)


def _attn_sample_kernel(scal_ref, q_ref, kp_ref, vp_ref, kn_ref, vn_ref, g_ref, sg_ref, o_ref, *, past, steps):
    h = pl.program_id(1)
    lam = scal_ref[0]
    lam_scale = scal_ref[1]
    slope = scal_ref[2 + h]

    q_lo, q_hi = _masked_halves(q_ref[...])
    kp = kp_ref[...].astype(BF16)
    kn = kn_ref[...].astype(BF16)

    def bias(n_keys, first_key):
        pos_q = past + lax.broadcasted_iota(jnp.int32, (steps, n_keys), 0)
        pos_k = first_key + lax.broadcasted_iota(jnp.int32, (steps, n_keys), 1)
        visible = (pos_k // CHUNK) <= (pos_q // CHUNK)
        return jnp.where(visible, -slope * jnp.abs(pos_q - pos_k).astype(F32), NEG_INF)

    bias_p = bias(past, 0)
    bias_n = bias(steps, past)

    def softmax(qq):
        s_p = _nt_dot(qq, kp) + bias_p
        s_n = _nt_dot(qq, kn) + bias_n
        m = jnp.maximum(jnp.max(s_p, axis=-1, keepdims=True), jnp.max(s_n, axis=-1, keepdims=True))
        e_p = jnp.exp(s_p - m)
        e_n = jnp.exp(s_n - m)
        l = jnp.sum(e_p, axis=-1, keepdims=True) + jnp.sum(e_n, axis=-1, keepdims=True)
        return e_p / l, e_n / l

    p1_p, p1_n = softmax(q_lo)
    p2_p, p2_n = softmax(q_hi)
    w_p = (p1_p - lam * p2_p).astype(BF16)
    w_n = (p1_n - lam * p2_n).astype(BF16)
    o = (jnp.dot(w_p, vp_ref[...].astype(BF16), preferred_element_type=F32)
         + jnp.dot(w_n, vn_ref[...].astype(BF16), preferred_element_type=F32))
    o_ref[...] = _subln_gate(o, sg_ref[...], lam_scale, g_ref[...])


def _attn_sample(proj, cache_k, cache_v, layer, scal, subln_g, batch, steps):
    past = cache_k.shape[2]
    kernel = functools.partial(_attn_sample_kernel, past=past, steps=steps)
    hb = HEAD_BLOCKS
    return pl.pallas_call(
        kernel,
        grid=(batch, A_HEADS),
        in_specs=[pl.BlockSpec(memory_space=pltpu.SMEM),
                  pl.BlockSpec((None, steps, A_V), lambda b, h: (b, 0, COL_Q * hb + h)),
                  pl.BlockSpec((None, None, past, A_V), lambda b, h: (layer, b, 0, h)),
                  pl.BlockSpec((None, None, past, A_V), lambda b, h: (layer, b, 0, h)),
                  pl.BlockSpec((None, steps, A_V), lambda b, h: (b, 0, COL_K * hb + h)),
                  pl.BlockSpec((None, steps, A_V), lambda b, h: (b, 0, COL_V * hb + h)),
                  pl.BlockSpec((None, steps, A_V), lambda b, h: (b, 0, COL_GA * hb + h)),
                  pl.BlockSpec((1, A_V), lambda b, h: (0, 0))],
        out_specs=pl.BlockSpec((None, steps, A_V), lambda b, h: (b, 0, h)),
        out_shape=jax.ShapeDtypeStruct((batch, steps, D_MODEL), F32),
        compiler_params=_params("parallel", "parallel"),
        name="attn_sample",
    )(scal, proj, cache_k, cache_v, proj, proj, proj, subln_g)


CONV_PAD = 8


def _rglru_kernel(xr_ref, gb_ref, h0_ref, cbuf_ref, cw_ref, cb_ref, wa_ref, ba_ref, wx_ref, bx_ref, lam_ref,
                  ob_ref, hl_ref, xp_ref, a_ref, b_ref, h_ref, hc_ref, *, tt):
    t = pl.program_id(1)
    tail = CONV_W - 1

    @pl.when(t == 0)
    def _():
        xp_ref[CONV_PAD - tail:CONV_PAD, :] = cbuf_ref[...]
        hc_ref[...] = h0_ref[...]

    x = xr_ref[...]
    xp_ref[CONV_PAD:CONV_PAD + tt, :] = x
    acc = xp_ref[CONV_PAD - 3:CONV_PAD - 3 + tt, :] * cw_ref[0:1, :]
    acc = acc + xp_ref[CONV_PAD - 2:CONV_PAD - 2 + tt, :] * cw_ref[1:2, :]
    acc = acc + xp_ref[CONV_PAD - 1:CONV_PAD - 1 + tt, :] * cw_ref[2:3, :]
    acc = acc + x * cw_ref[3:4, :]
    xc = cb_ref[...] + acc
    xp_ref[CONV_PAD - tail:CONV_PAD, :] = xp_ref[CONV_PAD + tt - tail:CONV_PAD + tt, :]

    xcb = xc.astype(BF16)

    def block_diag(w_ref):
        return jnp.concatenate(
            [jnp.dot(xcb[:, n * R_BS:(n + 1) * R_BS], w_ref[n], preferred_element_type=F32)
             for n in range(R_BLOCKS)], axis=-1)

    r = jax.nn.sigmoid(block_diag(wa_ref) + ba_ref[...])
    gate_i = jax.nn.sigmoid(block_diag(wx_ref) + bx_ref[...])
    neg_lam = -lam_ref[...]
    softplus = jnp.maximum(neg_lam, 0.0) + jnp.log1p(jnp.exp(-jnp.abs(neg_lam)))
    log_a = -RG_C * r * softplus
    a = jnp.exp(log_a)
    a_ref[...] = a
    b_ref[...] = jnp.sqrt(-jnp.tanh(log_a) * (a * a + 1.0)) * (gate_i * xc)

    def scan_row(s, h):
        h = a_ref[pl.ds(s, 1), :] * h + b_ref[pl.ds(s, 1), :]
        h_ref[pl.ds(s, 1), :] = h
        return h

    h_last = lax.fori_loop(0, tt, scan_row, hc_ref[...], unroll=8)
    hc_ref[...] = h_last
    hl_ref[...] = h_last
    ob_ref[...] = h_ref[...] * _silu(gb_ref[...])


def _rglru(proj, h0, conv_buf, lw, batch, seq):
    tt = min(seq, 256)
    assert seq % tt == 0 and tt % 8 == 0
    kernel = functools.partial(_rglru_kernel, tt=tt)
    row = pl.BlockSpec((1, R_W), lambda b, t: (0, 0))
    wblk = pl.BlockSpec((R_BLOCKS, R_BS, R_BS), lambda b, t: (0, 0, 0))
    return pl.pallas_call(
        kernel,
        grid=(batch, seq // tt),
        in_specs=[pl.BlockSpec((None, tt, R_W), lambda b, t: (b, t, COL_XR)),
                  pl.BlockSpec((None, tt, R_W), lambda b, t: (b, t, COL_GB)),
                  pl.BlockSpec((None, 1, R_W), lambda b, t: (b, 0, 0)),
                  pl.BlockSpec((None, CONV_W - 1, R_W), lambda b, t: (b, 0, 0)),
                  pl.BlockSpec((CONV_W, R_W), lambda b, t: (0, 0)),
                  row, wblk, row, wblk, row, row],
        out_specs=[pl.BlockSpec((None, tt, R_W), lambda b, t: (b, t, 0)),
                   pl.BlockSpec((None, 1, R_W), lambda b, t: (b, 0, 0))],
        out_shape=[jax.ShapeDtypeStruct((batch, seq, R_W), F32),
                   jax.ShapeDtypeStruct((batch, 1, R_W), F32)],
        scratch_shapes=[pltpu.VMEM((CONV_PAD + tt, R_W), F32),
                        pltpu.VMEM((tt, R_W), F32), pltpu.VMEM((tt, R_W), F32), pltpu.VMEM((tt, R_W), F32),
                        pltpu.VMEM((1, R_W), F32)],
        compiler_params=_params("parallel", "arbitrary"),
        name="conv_rglru",
    )(proj, proj, h0, conv_buf, lw["conv_w"], lw["conv_b"], lw["rg_wa"], lw["rg_ba"], lw["rg_wx"], lw["rg_bx"],
      lw["rg_lambda"])


def _cross_attention(qm, gc, mk_ref, mv_ref):
    parts = []
    for hd in range(M_HEADS):
        sl = slice(hd * M_HD, (hd + 1) * M_HD)
        qh = (qm[:, sl] * (M_HD ** -0.5)).astype(BF16)
        s = _nt_dot(qh, mk_ref[:, sl].astype(BF16))
        e = jnp.exp(s - jnp.max(s, axis=-1, keepdims=True))
        p = (e / jnp.sum(e, axis=-1, keepdims=True)).astype(BF16)
        oh = jnp.dot(p, mv_ref[:, sl].astype(BF16), preferred_element_type=F32)
        parts.append(oh * _silu(gc[:, sl]))
    return jnp.concatenate(parts, axis=-1)


def _cross_kernel(qm_ref, gc_ref, mk_ref, mv_ref, oc_ref):
    oc_ref[...] = _cross_attention(qm_ref[...], gc_ref[...], mk_ref, mv_ref)


def _cross_sample(proj, mem_k, mem_v, layer, batch, steps):
    m_tok = mem_k.shape[2]
    return pl.pallas_call(
        _cross_kernel,
        grid=(batch,),
        in_specs=[pl.BlockSpec((None, steps, D_MODEL), lambda b: (b, 0, COL_QM)),
                  pl.BlockSpec((None, steps, D_MODEL), lambda b: (b, 0, COL_GC)),
                  pl.BlockSpec((None, None, m_tok, D_MODEL), lambda b: (layer, b, 0, 0)),
                  pl.BlockSpec((None, None, m_tok, D_MODEL), lambda b: (layer, b, 0, 0))],
        out_specs=pl.BlockSpec((None, steps, D_MODEL), lambda b: (b, 0, 0)),
        out_shape=jax.ShapeDtypeStruct((batch, steps, D_MODEL), F32),
        compiler_params=_params("parallel"),
        name="cross_sample",
    )(proj, proj, mem_k, mem_v)


def _merge_kernel(*refs, alpha, fused_cross):
    if fused_cross:
        oa_ref, ob_ref, qm_ref, gc_ref, mk_ref, mv_ref, *refs = refs
        oc = _cross_attention(qm_ref[...], gc_ref[...], mk_ref, mv_ref)
    else:
        oa_ref, ob_ref, oc_ref, *refs = refs
        oc = oc_ref[...]
    *gm_refs, x_ref, wb_ref, wo_ref, lng_ref, lnb_ref, y_ref = refs
    m = None
    for n, o in enumerate((oa_ref[...], ob_ref[...], oc)):
        gate = jax.nn.sigmoid(gm_refs[n][...])
        term = gate * jnp.dot(o.astype(BF16), wb_ref[n], preferred_element_type=F32)
        m = term if m is None else m + term
    out = jnp.dot(m.astype(BF16), wo_ref[...], preferred_element_type=F32)
    y_ref[...] = _layer_norm_rows(alpha * x_ref[...] + out, lng_ref[...], lnb_ref[...])


def _merge(proj2d, o_a, o_b, x, lw, alpha, *, o_c=None, mem_k=None, mem_v=None, seq=None):
    n = x.shape[0]
    fused = o_c is None
    tm = min(n, 256)
    tok = lambda c: pl.BlockSpec((tm, D_MODEL), lambda i: (i, c))
    full = lambda shape: pl.BlockSpec(shape, lambda i: (0,) * len(shape))
    in_specs = [tok(0), tok(0)]
    args = [o_a, o_b]
    if fused:
        assert seq % tm == 0
        per_batch = seq // tm
        m_tok = mem_k.shape[1]
        mem_spec = pl.BlockSpec((None, m_tok, D_MODEL), lambda i: (i // per_batch, 0, 0))
        in_specs += [tok(COL_QM), tok(COL_GC), mem_spec, mem_spec]
        args += [proj2d, proj2d, mem_k, mem_v]
    else:
        in_specs += [tok(0)]
        args += [o_c]
    in_specs += [tok(COL_GM + n) for n in range(N_BRANCH)]
    in_specs += [tok(0),
                 full((N_BRANCH, D_MODEL, D_MODEL)), full((D_MODEL, D_MODEL)),
                 full((1, D_MODEL)), full((1, D_MODEL))]
    args += [proj2d] * N_BRANCH + [x, lw["w_branch"], lw["w_o"], lw["ln_g"], lw["ln_b"]]
    kernel = functools.partial(_merge_kernel, alpha=alpha, fused_cross=fused)
    return pl.pallas_call(
        kernel,
        grid=(n // tm,),
        in_specs=in_specs,
        out_specs=pl.BlockSpec((tm, D_MODEL), lambda i: (i, 0)),
        out_shape=jax.ShapeDtypeStruct((n, D_MODEL), F32),
        compiler_params=_params("parallel"),
        name="merge_fused" if fused else "merge",
    )(*args)


def kernel(x_prompt, x_sample, cache_k, cache_v, cache_mem_k, cache_mem_v, state_rnn_h, state_conv, mem_prompt,
           ln_in_g, ln_in_b, w_in, lambda_q1, lambda_k1, lambda_q2, lambda_k2, subln_g, conv_w, conv_b, rg_wa,
           rg_ba, rg_wx, rg_bx, rg_lambda, w_mem_kv, w_branch, w_o, ln_g, ln_b):
    bp, sp, _ = x_prompt.shape
    bs, ss, _ = x_sample.shape
    depth = w_in.shape[0]
    past = cache_k.shape[2]
    m_tok = mem_prompt.shape[1]
    alpha = (2 * depth) ** 0.25
    assert sp >= CONV_W - 1 and ss >= CONV_W - 1
    assert w_in.shape[2] == IN_COLS

    w_in_b = w_in.astype(BF16)
    w_mem_b = w_mem_kv.astype(BF16)
    w_branch_b = w_branch.astype(BF16)
    w_o_b = w_o.astype(BF16)
    rg_wa_b = rg_wa.astype(BF16)
    rg_wx_b = rg_wx.astype(BF16)
    cache_k2 = cache_k.reshape(depth, bs, past, D_MODEL)
    cache_v2 = cache_v.reshape(depth, bs, past, D_MODEL)
    cmem_k2 = cache_mem_k.reshape(depth, bs, m_tok, D_MODEL)
    cmem_v2 = cache_mem_v.reshape(depth, bs, m_tok, D_MODEL)
    slopes = jnp.exp2(-8.0 * jnp.arange(1, A_HEADS + 1, dtype=F32) / A_HEADS)

    xp = _input_ln(x_prompt.reshape(bp * sp, D_MODEL), ln_in_g, ln_in_b)
    xs = _input_ln(x_sample.reshape(bs * ss, D_MODEL), ln_in_g, ln_in_b)
    mem2d = mem_prompt.reshape(bp * m_tok, D_MODEL)
    h0_p = jnp.zeros((bp, 1, R_W), F32)
    buf0_p = jnp.zeros((bp, CONV_W - 1, R_W), F32)

    pk, pv, pmk, pmv, ph, pc = [], [], [], [], [], []
    sk, sv, sh, sc = [], [], [], []
    for l in range(depth):
        lam_init = 0.8 - 0.6 * math.exp(-0.3 * l)
        lam = (jnp.exp(jnp.sum(lambda_q1[l] * lambda_k1[l])) - jnp.exp(jnp.sum(lambda_q2[l] * lambda_k2[l]))
               + lam_init)
        scal = jnp.concatenate([jnp.stack([lam, jnp.asarray(1.0 - lam_init, F32)]), slopes]).astype(F32)
        sg = subln_g[l].reshape(1, A_V)
        lw = dict(conv_w=conv_w[l], conv_b=conv_b[l].reshape(1, R_W), rg_wa=rg_wa_b[l],
                  rg_ba=rg_ba[l].reshape(1, R_W), rg_wx=rg_wx_b[l], rg_bx=rg_bx[l].reshape(1, R_W),
                  rg_lambda=rg_lambda[l].reshape(1, R_W), w_branch=w_branch_b[l], w_o=w_o_b[l],
                  ln_g=ln_g[l].reshape(1, D_MODEL), ln_b=ln_b[l].reshape(1, D_MODEL))

        mkv = _project(mem2d, w_mem_b, l)
        mk = mkv[:, :D_MODEL].reshape(bp, m_tok, D_MODEL)
        mv = mkv[:, D_MODEL:].reshape(bp, m_tok, D_MODEL)
        proj = _project(xp, w_in_b, l)
        proj3 = proj.reshape(bp, sp, IN_COLS)
        o_a = _attn_prompt(proj3, scal, sg, bp, sp)
        o_b, h_new = _rglru(proj3, h0_p, buf0_p, lw, bp, sp)
        xp = _merge(proj, o_a.reshape(bp * sp, D_MODEL), o_b.reshape(bp * sp, D_MODEL), xp, lw, alpha,
                    mem_k=mk, mem_v=mv, seq=sp)
        pk.append(proj3[:, :, COL_K * D_MODEL:(COL_K + 1) * D_MODEL].reshape(bp, sp, A_HEADS, 2 * A_QK))
        pv.append(proj3[:, :, COL_V * D_MODEL:(COL_V + 1) * D_MODEL].reshape(bp, sp, A_HEADS, A_V))
        pmk.append(mk.reshape(bp, m_tok, M_HEADS, M_HD))
        pmv.append(mv.reshape(bp, m_tok, M_HEADS, M_HD))
        ph.append(h_new.reshape(bp, R_W))
        pc.append(proj3[:, sp - (CONV_W - 1):, COL_XR * D_MODEL:(COL_XR + 1) * D_MODEL])

        proj = _project(xs, w_in_b, l)
        proj3 = proj.reshape(bs, ss, IN_COLS)
        o_a = _attn_sample(proj3, cache_k2, cache_v2, l, scal, sg, bs, ss)
        o_b, h_new = _rglru(proj3, state_rnn_h[l].reshape(bs, 1, R_W), state_conv[l], lw, bs, ss)
        o_c = _cross_sample(proj3, cmem_k2, cmem_v2, l, bs, ss)
        xs = _merge(proj, o_a.reshape(bs * ss, D_MODEL), o_b.reshape(bs * ss, D_MODEL), xs, lw, alpha,
                    o_c=o_c.reshape(bs * ss, D_MODEL))
        sk.append(proj3[:, :, COL_K * D_MODEL:(COL_K + 1) * D_MODEL].reshape(bs, ss, A_HEADS, 2 * A_QK))
        sv.append(proj3[:, :, COL_V * D_MODEL:(COL_V + 1) * D_MODEL].reshape(bs, ss, A_HEADS, A_V))
        sh.append(h_new.reshape(bs, R_W))
        sc.append(proj3[:, ss - (CONV_W - 1):, COL_XR * D_MODEL:(COL_XR + 1) * D_MODEL])

    return (xp.reshape(bp, sp, D_MODEL), xs.reshape(bs, ss, D_MODEL),
            jnp.stack(pk), jnp.stack(pv), jnp.stack(pmk), jnp.stack(pmv), jnp.stack(ph), jnp.stack(pc),
            jnp.stack(sk), jnp.stack(sv), jnp.stack(sh), jnp.stack(sc))
```

```python
import functools
import math

import jax
import jax.numpy as jnp
from jax import lax
from jax.experimental import pallas as pl
from jax.experimental.pallas import tpu as pltpu

F32 = jnp.float32
BF16 = jnp.bfloat16

D_MODEL = 1024
CHUNK = 64
A_HEADS = 8
A_QK = 64
A_V = 2 * A_QK
R_W = D_MODEL
R_BLOCKS = 8
R_BS = R_W // R_BLOCKS
CONV_W = 4
RG_C = 8.0
M_HEADS = 4
M_HD = 256
N_BRANCH = 3
IN_COLS = 11 * D_MODEL
EPS = 1e-5
NEG_INF = -1e30
SEC_Q, SEC_K, SEC_V, SEC_GA, SEC_XR, SEC_GB, SEC_QM, SEC_GC, SEC_GM = range(9)
BF_SECTIONS = (SEC_Q, SEC_QM)
QCOL_Q, QCOL_QM = 0, 1
COL_K, COL_V, COL_GA, COL_XR, COL_GB, COL_GC, COL_GM = range(7)
HEAD_BLOCKS = D_MODEL // A_V

VMEM_LIMIT = 56 * 1024 * 1024
POS_SPLIT = 64


def _params(*sem):
    return pltpu.CompilerParams(dimension_semantics=sem, vmem_limit_bytes=VMEM_LIMIT)


def _nt_dot(a, b):
    return lax.dot_general(a, b, (((1,), (1,)), ((), ())), preferred_element_type=F32)


def _silu(x):
    return x * jax.nn.sigmoid(x)


def _layer_norm_rows(x, g, b):
    mu = jnp.mean(x, axis=-1, keepdims=True)
    xc = x - mu
    var = jnp.mean(xc * xc, axis=-1, keepdims=True)
    return xc * lax.rsqrt(var + EPS) * g + b


def _ln_kernel(x_ref, g_ref, b_ref, o_ref):
    o_ref[...] = _layer_norm_rows(x_ref[...], g_ref[...], b_ref[...])


def _input_ln(x, g, b):
    n = x.shape[0]
    tm = min(n, 1024)
    return pl.pallas_call(
        _ln_kernel,
        grid=(n // tm,),
        in_specs=[pl.BlockSpec((tm, D_MODEL), lambda i: (i, 0)),
                  pl.BlockSpec((1, D_MODEL), lambda i: (0, 0)),
                  pl.BlockSpec((1, D_MODEL), lambda i: (0, 0))],
        out_specs=pl.BlockSpec((tm, D_MODEL), lambda i: (i, 0)),
        out_shape=jax.ShapeDtypeStruct((n, D_MODEL), F32),
        compiler_params=_params("parallel"),
        name="input_ln",
    )(x, g.reshape(1, D_MODEL), b.reshape(1, D_MODEL))


def _proj_kernel(x_ref, w_ref, o_ref, xb_ref):
    @pl.when(pl.program_id(1) == 0)
    def _():
        xb_ref[...] = x_ref[...].astype(BF16)

    o_ref[...] = jnp.dot(xb_ref[...], w_ref[...], preferred_element_type=F32).astype(o_ref.dtype)


def _project(x, w_all, layer, out_dtype=F32):
    n = x.shape[0]
    cols = w_all.shape[2]
    tm = min(n, 1024)
    tn = D_MODEL
    return pl.pallas_call(
        _proj_kernel,
        grid=(n // tm, cols // tn),
        in_specs=[pl.BlockSpec((tm, D_MODEL), lambda i, j: (i, 0)),
                  pl.BlockSpec((None, D_MODEL, tn), lambda i, j: (layer, 0, j))],
        out_specs=pl.BlockSpec((tm, tn), lambda i, j: (i, j)),
        out_shape=jax.ShapeDtypeStruct((n, cols), out_dtype),
        scratch_shapes=[pltpu.VMEM((tm, D_MODEL), BF16)],
        compiler_params=_params("parallel", "arbitrary"),
        name="projection",
    )(x, w_all)


def _masked_halves(q):
    lane = lax.broadcasted_iota(jnp.int32, q.shape, 1)
    qs = q * (A_QK ** -0.5)
    zero = jnp.zeros_like(qs)
    return jnp.where(lane < A_QK, qs, zero), jnp.where(lane >= A_QK, qs, zero)


def _subln_gate(o, sg, lam_scale, g):
    o = o * lax.rsqrt(jnp.mean(o * o, axis=-1, keepdims=True) + EPS) * sg * lam_scale
    return o * _silu(g)


def _attn_prompt_kernel(scal_ref, q_ref, k_ref, v_ref, g_ref, sg_ref, o_ref,
                        kb_ref, vb_ref, corr_ref, qa_ref, sa_ref, sb_ref, m1_ref, a1_ref, m2_ref, a2_ref, *, tile):
    h = pl.program_id(1)
    i = pl.program_id(2)
    lam = scal_ref[0]
    lam_scale = scal_ref[1]
    slope = scal_ref[2 + h]
    seq = k_ref.shape[0]
    rep = tile // A_V

    @pl.when(i == 0)
    def _():
        pos = lax.broadcasted_iota(jnp.int32, (seq, A_V), 0)
        lane = lax.broadcasted_iota(jnp.int32, (seq, A_V), 1)
        hi = ((pos // POS_SPLIT) * POS_SPLIT).astype(F32) * slope
        lo = (pos % POS_SPLIT).astype(F32) * slope
        kb_ref[:, :A_V] = k_ref[...].astype(BF16)
        kb_ref[:, A_V:] = jnp.where(lane == 0, hi, jnp.where(lane == 1, lo, 0.0)).astype(BF16)
        vb_ref[:, :A_V] = v_ref[...].astype(BF16)
        vb_ref[:, A_V:] = jnp.ones((seq, A_V), BF16)
        row = lax.broadcasted_iota(jnp.int32, (tile, tile), 0)
        col = lax.broadcasted_iota(jnp.int32, (tile, tile), 1)
        visible = (col // CHUNK) <= (row // CHUNK)
        corr_ref[...] = jnp.where(visible, jnp.minimum(0.0, (2.0 * slope) * (row - col).astype(F32)), NEG_INF)

    q_lo, q_hi = _masked_halves(q_ref[...])
    lane = lax.broadcasted_iota(jnp.int32, (tile, A_V), 1)
    pos_cols = jnp.where(lane < 2, 1.0, 0.0).astype(BF16)
    qa_ref[0] = jnp.concatenate([q_lo, pos_cols], axis=-1)
    qa_ref[1] = jnp.concatenate([q_hi, pos_cols], axis=-1)

    for m_ref, a_ref in ((m1_ref, a1_ref), (m2_ref, a2_ref)):
        m_ref[...] = jnp.full(m_ref.shape, NEG_INF, F32)
        a_ref[...] = jnp.zeros(a_ref.shape, F32)

    def scores(j, s_ref):
        kt = kb_ref[pl.ds(pl.multiple_of(j * tile, tile), tile), :]
        for half in range(2):
            s_ref[half] = _nt_dot(qa_ref[half], kt)

    def consume(j, s_ref, diagonal):
        vt = vb_ref[pl.ds(pl.multiple_of(j * tile, tile), tile), :]
        for half, (m_ref, a_ref) in enumerate(((m1_ref, a1_ref), (m2_ref, a2_ref))):
            s = s_ref[half]
            if diagonal:
                s = s + corr_ref[...]
            m_prev = m_ref[...]
            m_new = jnp.maximum(m_prev, jnp.max(s, axis=-1, keepdims=True))
            alpha = jnp.exp(m_prev - m_new)
            p = jnp.exp(s - jnp.concatenate([m_new] * rep, axis=-1))
            pv = jnp.dot(p.astype(BF16), vt, preferred_element_type=F32)
            a_ref[...] = jnp.concatenate([alpha, alpha], axis=-1) * a_ref[...] + pv
            m_ref[...] = m_new

    scores(0, sa_ref)

    def pair(p, carry):
        j = 2 * p
        scores(j + 1, sb_ref)
        consume(j, sa_ref, False)
        scores(j + 2, sa_ref)
        consume(j + 1, sb_ref, False)
        return carry

    lax.fori_loop(0, i // 2, pair, 0)

    @pl.when(i % 2 == 1)
    def _():
        scores(i, sb_ref)
        consume(i - 1, sa_ref, False)
        consume(i, sb_ref, True)

    @pl.when(i % 2 == 0)
    def _():
        consume(i, sa_ref, True)

    o = (a1_ref[:, :A_V] / a1_ref[:, A_V:]) - lam * (a2_ref[:, :A_V] / a2_ref[:, A_V:])
    o_ref[...] = _subln_gate(o, sg_ref[...], lam_scale, g_ref[...])


def _attn_prompt(pq, pf, scal, subln_g, batch, seq):
    tile = min(seq, 512)
    assert seq % tile == 0 and tile % CHUNK == 0 and tile % A_V == 0
    assert seq <= POS_SPLIT * 256
    kernel = functools.partial(_attn_prompt_kernel, tile=tile)
    hb = HEAD_BLOCKS
    return pl.pallas_call(
        kernel,
        grid=(batch, A_HEADS, seq // tile),
        in_specs=[pl.BlockSpec(memory_space=pltpu.SMEM),
                  pl.BlockSpec((None, tile, A_V), lambda b, h, i: (b, i, QCOL_Q * hb + h)),
                  pl.BlockSpec((None, seq, A_V), lambda b, h, i: (b, 0, COL_K * hb + h)),
                  pl.BlockSpec((None, seq, A_V), lambda b, h, i: (b, 0, COL_V * hb + h)),
                  pl.BlockSpec((None, tile, A_V), lambda b, h, i: (b, i, COL_GA * hb + h)),
                  pl.BlockSpec((1, A_V), lambda b, h, i: (0, 0))],
        out_specs=pl.BlockSpec((None, tile, A_V), lambda b, h, i: (b, i, h)),
        out_shape=jax.ShapeDtypeStruct((batch, seq, D_MODEL), F32),
        scratch_shapes=[pltpu.VMEM((seq, 2 * A_V), BF16), pltpu.VMEM((seq, 2 * A_V), BF16),
                        pltpu.VMEM((tile, tile), F32),
                        pltpu.VMEM((2, tile, 2 * A_V), BF16),
                        pltpu.VMEM((2, tile, tile), F32), pltpu.VMEM((2, tile, tile), F32),
                        pltpu.VMEM((tile, A_V), F32), pltpu.VMEM((tile, 2 * A_V), F32),
                        pltpu.VMEM((tile, A_V), F32), pltpu.VMEM((tile, 2 * A_V), F32)],
        compiler_params=_params("parallel", "parallel", "arbitrary"),
        name="attn_prompt",
    )(scal, pq, pf, pf, pf, subln_g)


def _attn_sample_kernel(scal_ref, q_ref, kp_ref, vp_ref, kn_ref, vn_ref, g_ref, sg_ref, o_ref, *, past, steps):
    h = pl.program_id(1)
    lam = scal_ref[0]
    lam_scale = scal_ref[1]
    slope = scal_ref[2 + h]

    q_lo, q_hi = _masked_halves(q_ref[...])
    kp = kp_ref[...].astype(BF16)
    kn = kn_ref[...].astype(BF16)

    def bias(n_keys, first_key):
        pos_q = past + lax.broadcasted_iota(jnp.int32, (steps, n_keys), 0)
        pos_k = first_key + lax.broadcasted_iota(jnp.int32, (steps, n_keys), 1)
        visible = (pos_k // CHUNK) <= (pos_q // CHUNK)
        return jnp.where(visible, -slope * jnp.abs(pos_q - pos_k).astype(F32), NEG_INF)

    bias_p = bias(past, 0)
    bias_n = bias(steps, past)

    def softmax(qq):
        s_p = _nt_dot(qq, kp) + bias_p
        s_n = _nt_dot(qq, kn) + bias_n
        m = jnp.maximum(jnp.max(s_p, axis=-1, keepdims=True), jnp.max(s_n, axis=-1, keepdims=True))
        e_p = jnp.exp(s_p - m)
        e_n = jnp.exp(s_n - m)
        l = jnp.sum(e_p, axis=-1, keepdims=True) + jnp.sum(e_n, axis=-1, keepdims=True)
        return e_p / l, e_n / l

    p1_p, p1_n = softmax(q_lo)
    p2_p, p2_n = softmax(q_hi)
    w_p = (p1_p - lam * p2_p).astype(BF16)
    w_n = (p1_n - lam * p2_n).astype(BF16)
    o = (jnp.dot(w_p, vp_ref[...].astype(BF16), preferred_element_type=F32)
         + jnp.dot(w_n, vn_ref[...].astype(BF16), preferred_element_type=F32))
    o_ref[...] = _subln_gate(o, sg_ref[...], lam_scale, g_ref[...])


def _attn_sample(pq, pf, cache_k, cache_v, layer, scal, subln_g, batch, steps):
    past = cache_k.shape[2]
    kernel = functools.partial(_attn_sample_kernel, past=past, steps=steps)
    hb = HEAD_BLOCKS
    return pl.pallas_call(
        kernel,
        grid=(batch, A_HEADS),
        in_specs=[pl.BlockSpec(memory_space=pltpu.SMEM),
                  pl.BlockSpec((None, steps, A_V), lambda b, h: (b, 0, QCOL_Q * hb + h)),
                  pl.BlockSpec((None, None, past, A_V), lambda b, h: (layer, b, 0, h)),
                  pl.BlockSpec((None, None, past, A_V), lambda b, h: (layer, b, 0, h)),
                  pl.BlockSpec((None, steps, A_V), lambda b, h: (b, 0, COL_K * hb + h)),
                  pl.BlockSpec((None, steps, A_V), lambda b, h: (b, 0, COL_V * hb + h)),
                  pl.BlockSpec((None, steps, A_V), lambda b, h: (b, 0, COL_GA * hb + h)),
                  pl.BlockSpec((1, A_V), lambda b, h: (0, 0))],
        out_specs=pl.BlockSpec((None, steps, A_V), lambda b, h: (b, 0, h)),
        out_shape=jax.ShapeDtypeStruct((batch, steps, D_MODEL), F32),
        compiler_params=_params("parallel", "parallel"),
        name="attn_sample",
    )(scal, pq, cache_k, cache_v, pf, pf, pf, subln_g)


CONV_PAD = 8


def _rglru_kernel(xr_ref, gb_ref, h0_ref, cbuf_ref, cw_ref, cb_ref, wa_ref, ba_ref, wx_ref, bx_ref, lam_ref,
                  ob_ref, hl_ref, xp_ref, a_ref, b_ref, h_ref, hc_ref, *, tt):
    t = pl.program_id(1)
    tail = CONV_W - 1

    @pl.when(t == 0)
    def _():
        xp_ref[CONV_PAD - tail:CONV_PAD, :] = cbuf_ref[...]
        hc_ref[...] = h0_ref[...]

    x = xr_ref[...]
    xp_ref[CONV_PAD:CONV_PAD + tt, :] = x
    acc = xp_ref[CONV_PAD - 3:CONV_PAD - 3 + tt, :] * cw_ref[0:1, :]
    acc = acc + xp_ref[CONV_PAD - 2:CONV_PAD - 2 + tt, :] * cw_ref[1:2, :]
    acc = acc + xp_ref[CONV_PAD - 1:CONV_PAD - 1 + tt, :] * cw_ref[2:3, :]
    acc = acc + x * cw_ref[3:4, :]
    xc = cb_ref[...] + acc
    xp_ref[CONV_PAD - tail:CONV_PAD, :] = xp_ref[CONV_PAD + tt - tail:CONV_PAD + tt, :]

    xcb = xc.astype(BF16)

    def block_diag(w_ref):
        return jnp.concatenate(
            [jnp.dot(xcb[:, n * R_BS:(n + 1) * R_BS], w_ref[n], preferred_element_type=F32)
             for n in range(R_BLOCKS)], axis=-1)

    r = jax.nn.sigmoid(block_diag(wa_ref) + ba_ref[...])
    gate_i = jax.nn.sigmoid(block_diag(wx_ref) + bx_ref[...])
    neg_lam = -lam_ref[...]
    softplus = jnp.maximum(neg_lam, 0.0) + jnp.log1p(jnp.exp(-jnp.abs(neg_lam)))
    log_a = -RG_C * r * softplus
    a = jnp.exp(log_a)
    a_ref[...] = a
    b_ref[...] = jnp.sqrt(-jnp.tanh(log_a) * (a * a + 1.0)) * (gate_i * xc)

    def scan_row(s, h):
        h = a_ref[pl.ds(s, 1), :] * h + b_ref[pl.ds(s, 1), :]
        h_ref[pl.ds(s, 1), :] = h
        return h

    h_last = lax.fori_loop(0, tt, scan_row, hc_ref[...], unroll=8)
    hc_ref[...] = h_last
    hl_ref[...] = h_last
    ob_ref[...] = h_ref[...] * _silu(gb_ref[...])


def _rglru(pf, h0, conv_buf, lw, batch, seq):
    tt = min(seq, 256)
    assert seq % tt == 0 and tt % 8 == 0
    kernel = functools.partial(_rglru_kernel, tt=tt)
    row = pl.BlockSpec((1, R_W), lambda b, t: (0, 0))
    wblk = pl.BlockSpec((R_BLOCKS, R_BS, R_BS), lambda b, t: (0, 0, 0))
    return pl.pallas_call(
        kernel,
        grid=(batch, seq // tt),
        in_specs=[pl.BlockSpec((None, tt, R_W), lambda b, t: (b, t, COL_XR)),
                  pl.BlockSpec((None, tt, R_W), lambda b, t: (b, t, COL_GB)),
                  pl.BlockSpec((None, 1, R_W), lambda b, t: (b, 0, 0)),
                  pl.BlockSpec((None, CONV_W - 1, R_W), lambda b, t: (b, 0, 0)),
                  pl.BlockSpec((CONV_W, R_W), lambda b, t: (0, 0)),
                  row, wblk, row, wblk, row, row],
        out_specs=[pl.BlockSpec((None, tt, R_W), lambda b, t: (b, t, 0)),
                   pl.BlockSpec((None, 1, R_W), lambda b, t: (b, 0, 0))],
        out_shape=[jax.ShapeDtypeStruct((batch, seq, R_W), F32),
                   jax.ShapeDtypeStruct((batch, 1, R_W), F32)],
        scratch_shapes=[pltpu.VMEM((CONV_PAD + tt, R_W), F32),
                        pltpu.VMEM((tt, R_W), F32), pltpu.VMEM((tt, R_W), F32), pltpu.VMEM((tt, R_W), F32),
                        pltpu.VMEM((1, R_W), F32)],
        compiler_params=_params("parallel", "arbitrary"),
        name="conv_rglru",
    )(pf, pf, h0, conv_buf, lw["conv_w"], lw["conv_b"], lw["rg_wa"], lw["rg_ba"], lw["rg_wx"], lw["rg_bx"],
      lw["rg_lambda"])


def _cross_attention(qm, gc, mk_ref, mv_ref):
    parts = []
    for hd in range(M_HEADS):
        sl = slice(hd * M_HD, (hd + 1) * M_HD)
        qh = qm[:, sl] * (M_HD ** -0.5)
        s = _nt_dot(qh, mk_ref[:, sl].astype(BF16))
        e = jnp.exp(s - jnp.max(s, axis=-1, keepdims=True))
        p = (e / jnp.sum(e, axis=-1, keepdims=True)).astype(BF16)
        oh = jnp.dot(p, mv_ref[:, sl].astype(BF16), preferred_element_type=F32)
        parts.append(oh * _silu(gc[:, sl]))
    return jnp.concatenate(parts, axis=-1)


def _cross_kernel(qm_ref, gc_ref, mk_ref, mv_ref, oc_ref):
    oc_ref[...] = _cross_attention(qm_ref[...], gc_ref[...], mk_ref, mv_ref)


def _cross_sample(pq, pf, mem_k, mem_v, layer, batch, steps):
    m_tok = mem_k.shape[2]
    return pl.pallas_call(
        _cross_kernel,
        grid=(batch,),
        in_specs=[pl.BlockSpec((None, steps, D_MODEL), lambda b: (b, 0, QCOL_QM)),
                  pl.BlockSpec((None, steps, D_MODEL), lambda b: (b, 0, COL_GC)),
                  pl.BlockSpec((None, None, m_tok, D_MODEL), lambda b: (layer, b, 0, 0)),
                  pl.BlockSpec((None, None, m_tok, D_MODEL), lambda b: (layer, b, 0, 0))],
        out_specs=pl.BlockSpec((None, steps, D_MODEL), lambda b: (b, 0, 0)),
        out_shape=jax.ShapeDtypeStruct((batch, steps, D_MODEL), F32),
        compiler_params=_params("parallel"),
        name="cross_sample",
    )(pq, pf, mem_k, mem_v)


def _merge_kernel(*refs, alpha, fused_cross):
    if fused_cross:
        oa_ref, ob_ref, qm_ref, gc_ref, mk_ref, mv_ref, *refs = refs
        oc = _cross_attention(qm_ref[...], gc_ref[...], mk_ref, mv_ref)
    else:
        oa_ref, ob_ref, oc_ref, *refs = refs
        oc = oc_ref[...]
    *gm_refs, x_ref, wb_ref, wo_ref, lng_ref, lnb_ref, y_ref = refs
    m = None
    for n, o in enumerate((oa_ref[...], ob_ref[...], oc)):
        gate = jax.nn.sigmoid(gm_refs[n][...])
        term = gate * jnp.dot(o.astype(BF16), wb_ref[n], preferred_element_type=F32)
        m = term if m is None else m + term
    out = jnp.dot(m.astype(BF16), wo_ref[...], preferred_element_type=F32)
    y_ref[...] = _layer_norm_rows(alpha * x_ref[...] + out, lng_ref[...], lnb_ref[...])


def _merge(pq2d, pf2d, o_a, o_b, x, lw, alpha, *, o_c=None, mem_k=None, mem_v=None, seq=None):
    n = x.shape[0]
    fused = o_c is None
    tm = min(n, 256)
    tok = lambda c: pl.BlockSpec((tm, D_MODEL), lambda i: (i, c))
    full = lambda shape: pl.BlockSpec(shape, lambda i: (0,) * len(shape))
    in_specs = [tok(0), tok(0)]
    args = [o_a, o_b]
    if fused:
        assert seq % tm == 0
        per_batch = seq // tm
        m_tok = mem_k.shape[1]
        mem_spec = pl.BlockSpec((None, m_tok, D_MODEL), lambda i: (i // per_batch, 0, 0))
        in_specs += [tok(QCOL_QM), tok(COL_GC), mem_spec, mem_spec]
        args += [pq2d, pf2d, mem_k, mem_v]
    else:
        in_specs += [tok(0)]
        args += [o_c]
    in_specs += [tok(COL_GM + n) for n in range(N_BRANCH)]
    in_specs += [tok(0),
                 full((N_BRANCH, D_MODEL, D_MODEL)), full((D_MODEL, D_MODEL)),
                 full((1, D_MODEL)), full((1, D_MODEL))]
    args += [pf2d] * N_BRANCH + [x, lw["w_branch"], lw["w_o"], lw["ln_g"], lw["ln_b"]]
    kernel = functools.partial(_merge_kernel, alpha=alpha, fused_cross=fused)
    return pl.pallas_call(
        kernel,
        grid=(n // tm,),
        in_specs=in_specs,
        out_specs=pl.BlockSpec((tm, D_MODEL), lambda i: (i, 0)),
        out_shape=jax.ShapeDtypeStruct((n, D_MODEL), F32),
        compiler_params=_params("parallel"),
        name="merge_fused" if fused else "merge",
    )(*args)


def _sections(w, secs):
    return jnp.concatenate([w[:, :, s * D_MODEL:(s + 1) * D_MODEL] for s in secs], axis=-1).astype(BF16)


def kernel(x_prompt, x_sample, cache_k, cache_v, cache_mem_k, cache_mem_v, state_rnn_h, state_conv, mem_prompt,
           ln_in_g, ln_in_b, w_in, lambda_q1, lambda_k1, lambda_q2, lambda_k2, subln_g, conv_w, conv_b, rg_wa,
           rg_ba, rg_wx, rg_bx, rg_lambda, w_mem_kv, w_branch, w_o, ln_g, ln_b):
    bp, sp, _ = x_prompt.shape
    bs, ss, _ = x_sample.shape
    depth = w_in.shape[0]
    past = cache_k.shape[2]
    m_tok = mem_prompt.shape[1]
    alpha = (2 * depth) ** 0.25
    assert sp >= CONV_W - 1 and ss >= CONV_W - 1
    assert w_in.shape[2] == IN_COLS

    w_q_b = _sections(w_in, BF_SECTIONS)
    w_f_b = jnp.concatenate([_sections(w_in, (SEC_K, SEC_V, SEC_GA, SEC_XR, SEC_GB, SEC_GC)),
                             w_in[:, :, SEC_GM * D_MODEL:].astype(BF16)], axis=-1)
    w_mem_b = w_mem_kv.astype(BF16)
    w_branch_b = w_branch.astype(BF16)
    w_o_b = w_o.astype(BF16)
    rg_wa_b = rg_wa.astype(BF16)
    rg_wx_b = rg_wx.astype(BF16)
    cache_k2 = cache_k.reshape(depth, bs, past, D_MODEL)
    cache_v2 = cache_v.reshape(depth, bs, past, D_MODEL)
    cmem_k2 = cache_mem_k.reshape(depth, bs, m_tok, D_MODEL)
    cmem_v2 = cache_mem_v.reshape(depth, bs, m_tok, D_MODEL)
    slopes = jnp.exp2(-8.0 * jnp.arange(1, A_HEADS + 1, dtype=F32) / A_HEADS)

    xp = _input_ln(x_prompt.reshape(bp * sp, D_MODEL), ln_in_g, ln_in_b)
    xs = _input_ln(x_sample.reshape(bs * ss, D_MODEL), ln_in_g, ln_in_b)
    mem2d = mem_prompt.reshape(bp * m_tok, D_MODEL)
    h0_p = jnp.zeros((bp, 1, R_W), F32)
    buf0_p = jnp.zeros((bp, CONV_W - 1, R_W), F32)
    nq = len(BF_SECTIONS) * D_MODEL
    nf = IN_COLS - nq

    def sec(p3, col):
        return p3[:, :, col * D_MODEL:(col + 1) * D_MODEL]

    pk, pv, pmk, pmv, ph, pc = [], [], [], [], [], []
    sk, sv, sh, sc = [], [], [], []
    for l in range(depth):
        lam_init = 0.8 - 0.6 * math.exp(-0.3 * l)
        lam = (jnp.exp(jnp.sum(lambda_q1[l] * lambda_k1[l])) - jnp.exp(jnp.sum(lambda_q2[l] * lambda_k2[l]))
               + lam_init)
        scal = jnp.concatenate([jnp.stack([lam, jnp.asarray(1.0 - lam_init, F32)]), slopes]).astype(F32)
        sg = subln_g[l].reshape(1, A_V)
        lw = dict(conv_w=conv_w[l], conv_b=conv_b[l].reshape(1, R_W), rg_wa=rg_wa_b[l],
                  rg_ba=rg_ba[l].reshape(1, R_W), rg_wx=rg_wx_b[l], rg_bx=rg_bx[l].reshape(1, R_W),
                  rg_lambda=rg_lambda[l].reshape(1, R_W), w_branch=w_branch_b[l], w_o=w_o_b[l],
                  ln_g=ln_g[l].reshape(1, D_MODEL), ln_b=ln_b[l].reshape(1, D_MODEL))

        mkv = _project(mem2d, w_mem_b, l)
        mk = mkv[:, :D_MODEL].reshape(bp, m_tok, D_MODEL)
        mv = mkv[:, D_MODEL:].reshape(bp, m_tok, D_MODEL)
        pq = _project(xp, w_q_b, l, BF16)
        pf = _project(xp, w_f_b, l)
        pq3 = pq.reshape(bp, sp, nq)
        pf3 = pf.reshape(bp, sp, nf)
        o_a = _attn_prompt(pq3, pf3, scal, sg, bp, sp)
        o_b, h_new = _rglru(pf3, h0_p, buf0_p, lw, bp, sp)
        xp = _merge(pq, pf, o_a.reshape(bp * sp, D_MODEL), o_b.reshape(bp * sp, D_MODEL), xp, lw, alpha,
                    mem_k=mk, mem_v=mv, seq=sp)
        pk.append(sec(pf3, COL_K).reshape(bp, sp, A_HEADS, 2 * A_QK))
        pv.append(sec(pf3, COL_V).reshape(bp, sp, A_HEADS, A_V))
        pmk.append(mk.reshape(bp, m_tok, M_HEADS, M_HD))
        pmv.append(mv.reshape(bp, m_tok, M_HEADS, M_HD))
        ph.append(h_new.reshape(bp, R_W))
        pc.append(sec(pf3, COL_XR)[:, sp - (CONV_W - 1):])

        pq = _project(xs, w_q_b, l, BF16)
        pf = _project(xs, w_f_b, l)
        pq3 = pq.reshape(bs, ss, nq)
        pf3 = pf.reshape(bs, ss, nf)
        o_a = _attn_sample(pq3, pf3, cache_k2, cache_v2, l, scal, sg, bs, ss)
        o_b, h_new = _rglru(pf3, state_rnn_h[l].reshape(bs, 1, R_W), state_conv[l], lw, bs, ss)
        o_c = _cross_sample(pq3, pf3, cmem_k2, cmem_v2, l, bs, ss)
        xs = _merge(pq, pf, o_a.reshape(bs * ss, D_MODEL), o_b.reshape(bs * ss, D_MODEL), xs, lw, alpha,
                    o_c=o_c.reshape(bs * ss, D_MODEL))
        sk.append(sec(pf3, COL_K).reshape(bs, ss, A_HEADS, 2 * A_QK))
        sv.append(sec(pf3, COL_V).reshape(bs, ss, A_HEADS, A_V))
        sh.append(h_new.reshape(bs, R_W))
        sc.append(sec(pf3, COL_XR)[:, ss - (CONV_W - 1):])

    return (xp.reshape(bp, sp, D_MODEL), xs.reshape(bs, ss, D_MODEL),
            jnp.stack(pk), jnp.stack(pv), jnp.stack(pmk), jnp.stack(pmv), jnp.stack(ph), jnp.stack(pc),
            jnp.stack(sk), jnp.stack(sv), jnp.stack(sh), jnp.stack(sc))
```

```python
import functools
import math

import jax
import jax.numpy as jnp
from jax import lax
from jax.experimental import pallas as pl
from jax.experimental.pallas import tpu as pltpu

F32 = jnp.float32
BF16 = jnp.bfloat16

D_MODEL = 1024
CHUNK = 64
A_HEADS = 8
A_QK = 64
A_V = 2 * A_QK
R_W = D_MODEL
R_BLOCKS = 8
R_BS = R_W // R_BLOCKS
CONV_W = 4
RG_C = 8.0
M_HEADS = 4
M_HD = 256
N_BRANCH = 3
IN_COLS = 11 * D_MODEL
EPS = 1e-5
NEG_INF = -1e30
SEC_Q, SEC_K, SEC_V, SEC_GA, SEC_XR, SEC_GB, SEC_QM, SEC_GC, SEC_GM = range(9)
BF_SECTIONS = (SEC_Q, SEC_QM)
QCOL_Q, QCOL_QM = 0, 1
F_SECTIONS = (SEC_GA, SEC_XR, SEC_GB, SEC_GC)
COL_GA, COL_XR, COL_GB, COL_GC, COL_GM = range(5)
KVCOL_K, KVCOL_V = 0, 1
HEAD_BLOCKS = D_MODEL // A_V

VMEM_LIMIT = 56 * 1024 * 1024
POS_SPLIT = 64


def _params(*sem):
    return pltpu.CompilerParams(dimension_semantics=sem, vmem_limit_bytes=VMEM_LIMIT)


def _nt_dot(a, b):
    return lax.dot_general(a, b, (((1,), (1,)), ((), ())), preferred_element_type=F32)


def _silu(x):
    return x * jax.nn.sigmoid(x)


def _layer_norm_rows(x, g, b):
    mu = jnp.mean(x, axis=-1, keepdims=True)
    xc = x - mu
    var = jnp.mean(xc * xc, axis=-1, keepdims=True)
    return xc * lax.rsqrt(var + EPS) * g + b


def _ln_kernel(x_ref, g_ref, b_ref, o_ref):
    o_ref[...] = _layer_norm_rows(x_ref[...], g_ref[...], b_ref[...])


def _input_ln(x, g, b):
    n = x.shape[0]
    tm = min(n, 1024)
    return pl.pallas_call(
        _ln_kernel,
        grid=(n // tm,),
        in_specs=[pl.BlockSpec((tm, D_MODEL), lambda i: (i, 0)),
                  pl.BlockSpec((1, D_MODEL), lambda i: (0, 0)),
                  pl.BlockSpec((1, D_MODEL), lambda i: (0, 0))],
        out_specs=pl.BlockSpec((tm, D_MODEL), lambda i: (i, 0)),
        out_shape=jax.ShapeDtypeStruct((n, D_MODEL), F32),
        compiler_params=_params("parallel"),
        name="input_ln",
    )(x, g.reshape(1, D_MODEL), b.reshape(1, D_MODEL))


def _proj_kernel(x_ref, w_ref, o_ref, xb_ref):
    @pl.when(pl.program_id(1) == 0)
    def _():
        xb_ref[...] = x_ref[...].astype(BF16)

    o_ref[...] = jnp.dot(xb_ref[...], w_ref[...], preferred_element_type=F32).astype(o_ref.dtype)


def _project(x, w_all, layer, out_dtype=F32):
    n = x.shape[0]
    cols = w_all.shape[2]
    tm = min(n, 1024)
    tn = D_MODEL
    return pl.pallas_call(
        _proj_kernel,
        grid=(n // tm, cols // tn),
        in_specs=[pl.BlockSpec((tm, D_MODEL), lambda i, j: (i, 0)),
                  pl.BlockSpec((None, D_MODEL, tn), lambda i, j: (layer, 0, j))],
        out_specs=pl.BlockSpec((tm, tn), lambda i, j: (i, j)),
        out_shape=jax.ShapeDtypeStruct((n, cols), out_dtype),
        scratch_shapes=[pltpu.VMEM((tm, D_MODEL), BF16)],
        compiler_params=_params("parallel", "arbitrary"),
        name="projection",
    )(x, w_all)


def _kv_proj_kernel(x_ref, w_ref, k5_in, v5_in, kb_ref, vb_ref, k5_ref, v5_ref):
    del k5_in, v5_in
    tm = x_ref.shape[0]
    xb = x_ref[...].astype(BF16)
    for col, b_ref, o5_ref in ((KVCOL_K, kb_ref, k5_ref), (KVCOL_V, vb_ref, v5_ref)):
        res = jnp.dot(xb, w_ref[:, col * D_MODEL:(col + 1) * D_MODEL], preferred_element_type=F32)
        b_ref[...] = res.astype(BF16)
        for h in range(A_HEADS):
            o5_ref[pl.ds(h, tm, stride=A_HEADS), :] = res[:, h * A_V:(h + 1) * A_V]


def _kv_project(x, w_all, layer, k5, v5):
    n = x.shape[0]
    tm = min(n, 512)
    any_spec = pl.BlockSpec(memory_space=pl.ANY)
    tok = pl.BlockSpec((tm, D_MODEL), lambda i: (i, 0))
    out5 = pl.BlockSpec((None, tm * A_HEADS, A_V), lambda i: (layer, i, 0))
    return pl.pallas_call(
        _kv_proj_kernel,
        grid=(n // tm,),
        in_specs=[tok, pl.BlockSpec((None, D_MODEL, 2 * D_MODEL), lambda i: (layer, 0, 0)), any_spec, any_spec],
        out_specs=[tok, tok, out5, out5],
        out_shape=[jax.ShapeDtypeStruct((n, D_MODEL), BF16), jax.ShapeDtypeStruct((n, D_MODEL), BF16),
                   jax.ShapeDtypeStruct(k5.shape, F32), jax.ShapeDtypeStruct(v5.shape, F32)],
        input_output_aliases={2: 2, 3: 3},
        compiler_params=_params("parallel"),
        name="kv_projection",
    )(x, w_all, k5, v5)


def _masked_halves(q):
    lane = lax.broadcasted_iota(jnp.int32, q.shape, 1)
    qs = q * (A_QK ** -0.5)
    zero = jnp.zeros_like(qs)
    return jnp.where(lane < A_QK, qs, zero), jnp.where(lane >= A_QK, qs, zero)


def _subln_gate(o, sg, lam_scale, g):
    o = o * lax.rsqrt(jnp.mean(o * o, axis=-1, keepdims=True) + EPS) * sg * lam_scale
    return o * _silu(g)


def _attn_prompt_kernel(scal_ref, q_ref, k_ref, v_ref, g_ref, sg_ref, o_ref,
                        kb_ref, vb_ref, corr_ref, qa_ref, sa_ref, sb_ref, m1_ref, a1_ref, m2_ref, a2_ref, *, tile):
    h = pl.program_id(1)
    i = pl.program_id(2)
    lam = scal_ref[0]
    lam_scale = scal_ref[1]
    slope = scal_ref[2 + h]
    seq = k_ref.shape[0]
    rep = tile // A_V

    @pl.when(i == 0)
    def _():
        pos = lax.broadcasted_iota(jnp.int32, (seq, A_V), 0)
        lane = lax.broadcasted_iota(jnp.int32, (seq, A_V), 1)
        hi = ((pos // POS_SPLIT) * POS_SPLIT).astype(F32) * slope
        lo = (pos % POS_SPLIT).astype(F32) * slope
        kb_ref[:, :A_V] = k_ref[...]
        kb_ref[:, A_V:] = jnp.where(lane == 0, hi, jnp.where(lane == 1, lo, 0.0)).astype(BF16)
        vb_ref[:, :A_V] = v_ref[...]
        vb_ref[:, A_V:] = jnp.ones((seq, A_V), BF16)
        row = lax.broadcasted_iota(jnp.int32, (tile, tile), 0)
        col = lax.broadcasted_iota(jnp.int32, (tile, tile), 1)
        visible = (col // CHUNK) <= (row // CHUNK)
        corr_ref[...] = jnp.where(visible, jnp.minimum(0.0, (2.0 * slope) * (row - col).astype(F32)), NEG_INF)

    q_lo, q_hi = _masked_halves(q_ref[...])
    lane = lax.broadcasted_iota(jnp.int32, (tile, A_V), 1)
    pos_cols = jnp.where(lane < 2, 1.0, 0.0).astype(BF16)
    qa_ref[0] = jnp.concatenate([q_lo, pos_cols], axis=-1)
    qa_ref[1] = jnp.concatenate([q_hi, pos_cols], axis=-1)

    for m_ref, a_ref in ((m1_ref, a1_ref), (m2_ref, a2_ref)):
        m_ref[...] = jnp.full(m_ref.shape, NEG_INF, F32)
        a_ref[...] = jnp.zeros(a_ref.shape, F32)

    def scores(j, s_ref):
        kt = kb_ref[pl.ds(pl.multiple_of(j * tile, tile), tile), :]
        for half in range(2):
            s_ref[half] = _nt_dot(qa_ref[half], kt)

    def consume(j, s_ref, diagonal):
        vt = vb_ref[pl.ds(pl.multiple_of(j * tile, tile), tile), :]
        for half, (m_ref, a_ref) in enumerate(((m1_ref, a1_ref), (m2_ref, a2_ref))):
            s = s_ref[half]
            if diagonal:
                s = s + corr_ref[...]
            m_prev = m_ref[...]
            m_new = jnp.maximum(m_prev, jnp.max(s, axis=-1, keepdims=True))
            alpha = jnp.exp(m_prev - m_new)
            p = jnp.exp(s - jnp.concatenate([m_new] * rep, axis=-1))
            pv = jnp.dot(p.astype(BF16), vt, preferred_element_type=F32)
            a_ref[...] = jnp.concatenate([alpha, alpha], axis=-1) * a_ref[...] + pv
            m_ref[...] = m_new

    scores(0, sa_ref)

    def pair(p, carry):
        j = 2 * p
        scores(j + 1, sb_ref)
        consume(j, sa_ref, False)
        scores(j + 2, sa_ref)
        consume(j + 1, sb_ref, False)
        return carry

    lax.fori_loop(0, i // 2, pair, 0)

    @pl.when(i % 2 == 1)
    def _():
        scores(i, sb_ref)
        consume(i - 1, sa_ref, False)
        consume(i, sb_ref, True)

    @pl.when(i % 2 == 0)
    def _():
        consume(i, sa_ref, True)

    o = (a1_ref[:, :A_V] / a1_ref[:, A_V:]) - lam * (a2_ref[:, :A_V] / a2_ref[:, A_V:])
    o_ref[...] = _subln_gate(o, sg_ref[...], lam_scale, g_ref[...]).astype(o_ref.dtype)


def _attn_prompt(pq, kb, vb, pf, scal, subln_g, batch, seq):
    tile = min(seq, 512)
    assert seq % tile == 0 and tile % CHUNK == 0 and tile % A_V == 0
    assert seq <= POS_SPLIT * 256
    kernel = functools.partial(_attn_prompt_kernel, tile=tile)
    hb = HEAD_BLOCKS
    return pl.pallas_call(
        kernel,
        grid=(batch, A_HEADS, seq // tile),
        in_specs=[pl.BlockSpec(memory_space=pltpu.SMEM),
                  pl.BlockSpec((None, tile, A_V), lambda b, h, i: (b, i, QCOL_Q * hb + h)),
                  pl.BlockSpec((None, seq, A_V), lambda b, h, i: (b, 0, h)),
                  pl.BlockSpec((None, seq, A_V), lambda b, h, i: (b, 0, h)),
                  pl.BlockSpec((None, tile, A_V), lambda b, h, i: (b, i, COL_GA * hb + h)),
                  pl.BlockSpec((1, A_V), lambda b, h, i: (0, 0))],
        out_specs=pl.BlockSpec((None, tile, A_V), lambda b, h, i: (b, i, h)),
        out_shape=jax.ShapeDtypeStruct((batch, seq, D_MODEL), BF16),
        scratch_shapes=[pltpu.VMEM((seq, 2 * A_V), BF16), pltpu.VMEM((seq, 2 * A_V), BF16),
                        pltpu.VMEM((tile, tile), F32),
                        pltpu.VMEM((2, tile, 2 * A_V), BF16),
                        pltpu.VMEM((2, tile, tile), F32), pltpu.VMEM((2, tile, tile), F32),
                        pltpu.VMEM((tile, A_V), F32), pltpu.VMEM((tile, 2 * A_V), F32),
                        pltpu.VMEM((tile, A_V), F32), pltpu.VMEM((tile, 2 * A_V), F32)],
        compiler_params=_params("parallel", "parallel", "arbitrary"),
        name="attn_prompt",
    )(scal, pq, kb, vb, pf, subln_g)


def _attn_sample_kernel(scal_ref, q_ref, kp_ref, vp_ref, kv_ref, g_ref, sg_ref, o_ref, *, past, steps):
    lam = scal_ref[0]
    lam_scale = scal_ref[1]

    def geometry(n_keys, first_key):
        pos_q = past + lax.broadcasted_iota(jnp.int32, (steps, n_keys), 0)
        pos_k = first_key + lax.broadcasted_iota(jnp.int32, (steps, n_keys), 1)
        return (pos_k // CHUNK) <= (pos_q // CHUNK), jnp.abs(pos_q - pos_k).astype(F32)

    vis_p, dist_p = geometry(past, 0)
    vis_n, dist_n = geometry(steps, past)

    for h in range(A_HEADS):
        slope = scal_ref[2 + h]
        cols = slice(h * A_V, (h + 1) * A_V)
        q_lo, q_hi = _masked_halves(q_ref[:, cols])
        kp = kp_ref[pl.ds(h, past, stride=A_HEADS), :].astype(BF16)
        vp = vp_ref[pl.ds(h, past, stride=A_HEADS), :].astype(BF16)
        kn = kv_ref[:, KVCOL_K * D_MODEL + h * A_V:KVCOL_K * D_MODEL + (h + 1) * A_V].astype(BF16)
        vn = kv_ref[:, KVCOL_V * D_MODEL + h * A_V:KVCOL_V * D_MODEL + (h + 1) * A_V].astype(BF16)
        bias_p = jnp.where(vis_p, -slope * dist_p, NEG_INF)
        bias_n = jnp.where(vis_n, -slope * dist_n, NEG_INF)

        def softmax(qq):
            s_p = _nt_dot(qq, kp) + bias_p
            s_n = _nt_dot(qq, kn) + bias_n
            m = jnp.maximum(jnp.max(s_p, axis=-1, keepdims=True), jnp.max(s_n, axis=-1, keepdims=True))
            e_p = jnp.exp(s_p - m)
            e_n = jnp.exp(s_n - m)
            l = jnp.sum(e_p, axis=-1, keepdims=True) + jnp.sum(e_n, axis=-1, keepdims=True)
            return e_p / l, e_n / l

        p1_p, p1_n = softmax(q_lo)
        p2_p, p2_n = softmax(q_hi)
        w_p = (p1_p - lam * p2_p).astype(BF16)
        w_n = (p1_n - lam * p2_n).astype(BF16)
        o = jnp.dot(w_p, vp, preferred_element_type=F32) + jnp.dot(w_n, vn, preferred_element_type=F32)
        o_ref[:, cols] = _subln_gate(o, sg_ref[...], lam_scale, g_ref[:, cols]).astype(o_ref.dtype)


def _attn_sample(pq, pkv, pf, cache_k, cache_v, layer, scal, subln_g, batch, steps):
    past = cache_k.shape[2] // A_HEADS
    kernel = functools.partial(_attn_sample_kernel, past=past, steps=steps)
    cache_spec = pl.BlockSpec((None, None, past * A_HEADS, A_V), lambda b: (layer, b, 0, 0))
    return pl.pallas_call(
        kernel,
        grid=(batch,),
        in_specs=[pl.BlockSpec(memory_space=pltpu.SMEM),
                  pl.BlockSpec((None, steps, D_MODEL), lambda b: (b, 0, QCOL_Q)),
                  cache_spec, cache_spec,
                  pl.BlockSpec((None, steps, 2 * D_MODEL), lambda b: (b, 0, 0)),
                  pl.BlockSpec((None, steps, D_MODEL), lambda b: (b, 0, COL_GA)),
                  pl.BlockSpec((1, A_V), lambda b: (0, 0))],
        out_specs=pl.BlockSpec((None, steps, D_MODEL), lambda b: (b, 0, 0)),
        out_shape=jax.ShapeDtypeStruct((batch, steps, D_MODEL), BF16),
        compiler_params=_params("parallel"),
        name="attn_sample",
    )(scal, pq, cache_k, cache_v, pkv, pf, subln_g)


CONV_PAD = 8


def _rglru_kernel(xr_ref, gb_ref, h0_ref, cbuf_ref, cw_ref, cb_ref, wa_ref, ba_ref, wx_ref, bx_ref, lam_ref,
                  ob_ref, hl_ref, xp_ref, a_ref, b_ref, h_ref, hc_ref, *, tt):
    t = pl.program_id(1)
    tail = CONV_W - 1

    @pl.when(t == 0)
    def _():
        xp_ref[CONV_PAD - tail:CONV_PAD, :] = cbuf_ref[...]
        hc_ref[...] = h0_ref[...]

    x = xr_ref[...]
    xp_ref[CONV_PAD:CONV_PAD + tt, :] = x
    acc = xp_ref[CONV_PAD - 3:CONV_PAD - 3 + tt, :] * cw_ref[0:1, :]
    acc = acc + xp_ref[CONV_PAD - 2:CONV_PAD - 2 + tt, :] * cw_ref[1:2, :]
    acc = acc + xp_ref[CONV_PAD - 1:CONV_PAD - 1 + tt, :] * cw_ref[2:3, :]
    acc = acc + x * cw_ref[3:4, :]
    xc = cb_ref[...] + acc
    xp_ref[CONV_PAD - tail:CONV_PAD, :] = xp_ref[CONV_PAD + tt - tail:CONV_PAD + tt, :]

    xcb = xc.astype(BF16)

    def block_diag(w_ref):
        return jnp.concatenate(
            [jnp.dot(xcb[:, n * R_BS:(n + 1) * R_BS], w_ref[n], preferred_element_type=F32)
             for n in range(R_BLOCKS)], axis=-1)

    r = jax.nn.sigmoid(block_diag(wa_ref) + ba_ref[...])
    gate_i = jax.nn.sigmoid(block_diag(wx_ref) + bx_ref[...])
    neg_lam = -lam_ref[...]
    softplus = jnp.maximum(neg_lam, 0.0) + jnp.log1p(jnp.exp(-jnp.abs(neg_lam)))
    log_a = -RG_C * r * softplus
    a = jnp.exp(log_a)
    a_ref[...] = a
    b_ref[...] = jnp.sqrt(-jnp.tanh(log_a) * (a * a + 1.0)) * (gate_i * xc)

    def scan_row(s, h):
        h = a_ref[pl.ds(s, 1), :] * h + b_ref[pl.ds(s, 1), :]
        h_ref[pl.ds(s, 1), :] = h
        return h

    h_last = lax.fori_loop(0, tt, scan_row, hc_ref[...], unroll=8)
    hc_ref[...] = h_last
    hl_ref[...] = h_last
    ob_ref[...] = (h_ref[...] * _silu(gb_ref[...])).astype(ob_ref.dtype)


def _rglru(pf, h0, conv_buf, lw, batch, seq):
    tt = min(seq, 256)
    assert seq % tt == 0 and tt % 8 == 0
    kernel = functools.partial(_rglru_kernel, tt=tt)
    row = pl.BlockSpec((1, R_W), lambda b, t: (0, 0))
    wblk = pl.BlockSpec((R_BLOCKS, R_BS, R_BS), lambda b, t: (0, 0, 0))
    return pl.pallas_call(
        kernel,
        grid=(batch, seq // tt),
        in_specs=[pl.BlockSpec((None, tt, R_W), lambda b, t: (b, t, COL_XR)),
                  pl.BlockSpec((None, tt, R_W), lambda b, t: (b, t, COL_GB)),
                  pl.BlockSpec((None, 1, R_W), lambda b, t: (b, 0, 0)),
                  pl.BlockSpec((None, CONV_W - 1, R_W), lambda b, t: (b, 0, 0)),
                  pl.BlockSpec((CONV_W, R_W), lambda b, t: (0, 0)),
                  row, wblk, row, wblk, row, row],
        out_specs=[pl.BlockSpec((None, tt, R_W), lambda b, t: (b, t, 0)),
                   pl.BlockSpec((None, 1, R_W), lambda b, t: (b, 0, 0))],
        out_shape=[jax.ShapeDtypeStruct((batch, seq, R_W), BF16),
                   jax.ShapeDtypeStruct((batch, 1, R_W), F32)],
        scratch_shapes=[pltpu.VMEM((CONV_PAD + tt, R_W), F32),
                        pltpu.VMEM((tt, R_W), F32), pltpu.VMEM((tt, R_W), F32), pltpu.VMEM((tt, R_W), F32),
                        pltpu.VMEM((1, R_W), F32)],
        compiler_params=_params("parallel", "arbitrary"),
        name="conv_rglru",
    )(pf, pf, h0, conv_buf, lw["conv_w"], lw["conv_b"], lw["rg_wa"], lw["rg_ba"], lw["rg_wx"], lw["rg_bx"],
      lw["rg_lambda"])


def _cross_attention(qm, gc, mk_ref, mv_ref):
    parts = []
    for hd in range(M_HEADS):
        sl = slice(hd * M_HD, (hd + 1) * M_HD)
        qh = qm[:, sl] * (M_HD ** -0.5)
        s = _nt_dot(qh, mk_ref[:, sl].astype(BF16))
        e = jnp.exp(s - jnp.max(s, axis=-1, keepdims=True))
        p = (e / jnp.sum(e, axis=-1, keepdims=True)).astype(BF16)
        oh = jnp.dot(p, mv_ref[:, sl].astype(BF16), preferred_element_type=F32)
        parts.append(oh * _silu(gc[:, sl]))
    return jnp.concatenate(parts, axis=-1)


def _cross_kernel(qm_ref, gc_ref, mk_ref, mv_ref, oc_ref):
    oc_ref[...] = _cross_attention(qm_ref[...], gc_ref[...], mk_ref, mv_ref).astype(oc_ref.dtype)


def _cross_sample(pq, pf, mem_k, mem_v, layer, batch, steps):
    m_tok = mem_k.shape[2]
    return pl.pallas_call(
        _cross_kernel,
        grid=(batch,),
        in_specs=[pl.BlockSpec((None, steps, D_MODEL), lambda b: (b, 0, QCOL_QM)),
                  pl.BlockSpec((None, steps, D_MODEL), lambda b: (b, 0, COL_GC)),
                  pl.BlockSpec((None, None, m_tok, D_MODEL), lambda b: (layer, b, 0, 0)),
                  pl.BlockSpec((None, None, m_tok, D_MODEL), lambda b: (layer, b, 0, 0))],
        out_specs=pl.BlockSpec((None, steps, D_MODEL), lambda b: (b, 0, 0)),
        out_shape=jax.ShapeDtypeStruct((batch, steps, D_MODEL), BF16),
        compiler_params=_params("parallel"),
        name="cross_sample",
    )(pq, pf, mem_k, mem_v)


def _merge_kernel(*refs, alpha, fused_cross):
    if fused_cross:
        oa_ref, ob_ref, qm_ref, gc_ref, mk_ref, mv_ref, *refs = refs
        oc = _cross_attention(qm_ref[...], gc_ref[...], mk_ref, mv_ref)
    else:
        oa_ref, ob_ref, oc_ref, *refs = refs
        oc = oc_ref[...]
    *gm_refs, x_ref, wb_ref, wo_ref, lng_ref, lnb_ref, y_ref = refs
    m = None
    for n, o in enumerate((oa_ref[...], ob_ref[...], oc)):
        gate = jax.nn.sigmoid(gm_refs[n][...])
        term = gate * jnp.dot(o.astype(BF16), wb_ref[n], preferred_element_type=F32)
        m = term if m is None else m + term
    out = jnp.dot(m.astype(BF16), wo_ref[...], preferred_element_type=F32)
    y_ref[...] = _layer_norm_rows(alpha * x_ref[...] + out, lng_ref[...], lnb_ref[...])


def _merge(pq2d, pf2d, o_a, o_b, x, lw, alpha, *, o_c=None, mem_k=None, mem_v=None, seq=None):
    n = x.shape[0]
    fused = o_c is None
    tm = min(n, 256)
    tok = lambda c: pl.BlockSpec((tm, D_MODEL), lambda i: (i, c))
    full = lambda shape: pl.BlockSpec(shape, lambda i: (0,) * len(shape))
    in_specs = [tok(0), tok(0)]
    args = [o_a, o_b]
    if fused:
        assert seq % tm == 0
        per_batch = seq // tm
        m_tok = mem_k.shape[1]
        mem_spec = pl.BlockSpec((None, m_tok, D_MODEL), lambda i: (i // per_batch, 0, 0))
        in_specs += [tok(QCOL_QM), tok(COL_GC), mem_spec, mem_spec]
        args += [pq2d, pf2d, mem_k, mem_v]
    else:
        in_specs += [tok(0)]
        args += [o_c]
    in_specs += [tok(COL_GM + n) for n in range(N_BRANCH)]
    in_specs += [tok(0),
                 full((N_BRANCH, D_MODEL, D_MODEL)), full((D_MODEL, D_MODEL)),
                 full((1, D_MODEL)), full((1, D_MODEL))]
    args += [pf2d] * N_BRANCH + [x, lw["w_branch"], lw["w_o"], lw["ln_g"], lw["ln_b"]]
    kernel = functools.partial(_merge_kernel, alpha=alpha, fused_cross=fused)
    return pl.pallas_call(
        kernel,
        grid=(n // tm,),
        in_specs=in_specs,
        out_specs=pl.BlockSpec((tm, D_MODEL), lambda i: (i, 0)),
        out_shape=jax.ShapeDtypeStruct((n, D_MODEL), F32),
        compiler_params=_params("parallel"),
        name="merge_fused" if fused else "merge",
    )(*args)


def _sections(w, secs):
    return jnp.concatenate([w[:, :, s * D_MODEL:(s + 1) * D_MODEL] for s in secs], axis=-1).astype(BF16)


def kernel(x_prompt, x_sample, cache_k, cache_v, cache_mem_k, cache_mem_v, state_rnn_h, state_conv, mem_prompt,
           ln_in_g, ln_in_b, w_in, lambda_q1, lambda_k1, lambda_q2, lambda_k2, subln_g, conv_w, conv_b, rg_wa,
           rg_ba, rg_wx, rg_bx, rg_lambda, w_mem_kv, w_branch, w_o, ln_g, ln_b):
    bp, sp, _ = x_prompt.shape
    bs, ss, _ = x_sample.shape
    depth = w_in.shape[0]
    past = cache_k.shape[2]
    m_tok = mem_prompt.shape[1]
    alpha = (2 * depth) ** 0.25
    assert sp >= CONV_W - 1 and ss >= CONV_W - 1
    assert w_in.shape[2] == IN_COLS

    w_q_b = _sections(w_in, BF_SECTIONS)
    w_kv_b = _sections(w_in, (SEC_K, SEC_V))
    w_f_b = jnp.concatenate([_sections(w_in, F_SECTIONS), w_in[:, :, SEC_GM * D_MODEL:].astype(BF16)], axis=-1)
    w_mem_b = w_mem_kv.astype(BF16)
    w_branch_b = w_branch.astype(BF16)
    w_o_b = w_o.astype(BF16)
    rg_wa_b = rg_wa.astype(BF16)
    rg_wx_b = rg_wx.astype(BF16)
    cache_k2 = cache_k.reshape(depth, bs, past * A_HEADS, A_V)
    cache_v2 = cache_v.reshape(depth, bs, past * A_HEADS, A_V)
    cmem_k2 = cache_mem_k.reshape(depth, bs, m_tok, D_MODEL)
    cmem_v2 = cache_mem_v.reshape(depth, bs, m_tok, D_MODEL)
    slopes = jnp.exp2(-8.0 * jnp.arange(1, A_HEADS + 1, dtype=F32) / A_HEADS)

    xp = _input_ln(x_prompt.reshape(bp * sp, D_MODEL), ln_in_g, ln_in_b)
    xs = _input_ln(x_sample.reshape(bs * ss, D_MODEL), ln_in_g, ln_in_b)
    mem2d = mem_prompt.reshape(bp * m_tok, D_MODEL)
    h0_p = jnp.zeros((bp, 1, R_W), F32)
    buf0_p = jnp.zeros((bp, CONV_W - 1, R_W), F32)
    nq = len(BF_SECTIONS) * D_MODEL
    nf = (len(F_SECTIONS) + N_BRANCH) * D_MODEL
    k5 = jnp.zeros((depth, bp * sp * A_HEADS, A_V), F32)
    v5 = jnp.zeros((depth, bp * sp * A_HEADS, A_V), F32)

    def sec(p3, col):
        return p3[:, :, col * D_MODEL:(col + 1) * D_MODEL]

    pmk, pmv, ph, pc = [], [], [], []
    sk, sv, sh, sc = [], [], [], []
    for l in range(depth):
        lam_init = 0.8 - 0.6 * math.exp(-0.3 * l)
        lam = (jnp.exp(jnp.sum(lambda_q1[l] * lambda_k1[l])) - jnp.exp(jnp.sum(lambda_q2[l] * lambda_k2[l]))
               + lam_init)
        scal = jnp.concatenate([jnp.stack([lam, jnp.asarray(1.0 - lam_init, F32)]), slopes]).astype(F32)
        sg = subln_g[l].reshape(1, A_V)
        lw = dict(conv_w=conv_w[l], conv_b=conv_b[l].reshape(1, R_W), rg_wa=rg_wa_b[l],
                  rg_ba=rg_ba[l].reshape(1, R_W), rg_wx=rg_wx_b[l], rg_bx=rg_bx[l].reshape(1, R_W),
                  rg_lambda=rg_lambda[l].reshape(1, R_W), w_branch=w_branch_b[l], w_o=w_o_b[l],
                  ln_g=ln_g[l].reshape(1, D_MODEL), ln_b=ln_b[l].reshape(1, D_MODEL))

        mkv = _project(mem2d, w_mem_b, l)
        mk = mkv[:, :D_MODEL].reshape(bp, m_tok, D_MODEL)
        mv = mkv[:, D_MODEL:].reshape(bp, m_tok, D_MODEL)
        pq = _project(xp, w_q_b, l, BF16)
        pf = _project(xp, w_f_b, l)
        kb, vb, k5, v5 = _kv_project(xp, w_kv_b, l, k5, v5)
        pq3 = pq.reshape(bp, sp, nq)
        pf3 = pf.reshape(bp, sp, nf)
        o_a = _attn_prompt(pq3, kb.reshape(bp, sp, D_MODEL), vb.reshape(bp, sp, D_MODEL), pf3, scal, sg, bp, sp)
        o_b, h_new = _rglru(pf3, h0_p, buf0_p, lw, bp, sp)
        xp = _merge(pq, pf, o_a.reshape(bp * sp, D_MODEL), o_b.reshape(bp * sp, D_MODEL), xp, lw, alpha,
                    mem_k=mk, mem_v=mv, seq=sp)
        pmk.append(mk.reshape(bp, m_tok, M_HEADS, M_HD))
        pmv.append(mv.reshape(bp, m_tok, M_HEADS, M_HD))
        ph.append(h_new.reshape(bp, R_W))
        pc.append(sec(pf3, COL_XR)[:, sp - (CONV_W - 1):])

        pq = _project(xs, w_q_b, l, BF16)
        pf = _project(xs, w_f_b, l)
        pkv3 = _project(xs, w_kv_b, l).reshape(bs, ss, 2 * D_MODEL)
        pq3 = pq.reshape(bs, ss, nq)
        pf3 = pf.reshape(bs, ss, nf)
        o_a = _attn_sample(pq3, pkv3, pf3, cache_k2, cache_v2, l, scal, sg, bs, ss)
        o_b, h_new = _rglru(pf3, state_rnn_h[l].reshape(bs, 1, R_W), state_conv[l], lw, bs, ss)
        o_c = _cross_sample(pq3, pf3, cmem_k2, cmem_v2, l, bs, ss)
        xs = _merge(pq, pf, o_a.reshape(bs * ss, D_MODEL), o_b.reshape(bs * ss, D_MODEL), xs, lw, alpha,
                    o_c=o_c.reshape(bs * ss, D_MODEL))
        sk.append(sec(pkv3, KVCOL_K).reshape(bs, ss, A_HEADS, 2 * A_QK))
        sv.append(sec(pkv3, KVCOL_V).reshape(bs, ss, A_HEADS, A_V))
        sh.append(h_new.reshape(bs, R_W))
        sc.append(sec(pf3, COL_XR)[:, ss - (CONV_W - 1):])

    return (xp.reshape(bp, sp, D_MODEL), xs.reshape(bs, ss, D_MODEL),
            k5.reshape(depth, bp, sp, A_HEADS, 2 * A_QK), v5.reshape(depth, bp, sp, A_HEADS, A_V), jnp.stack(pmk), jnp.stack(pmv), jnp.stack(ph), jnp.stack(pc),
            jnp.stack(sk), jnp.stack(sv), jnp.stack(sh), jnp.stack(sc))
```

```python
import functools
import math

import jax
import jax.numpy as jnp
from jax import lax
from jax.experimental import pallas as pl
from jax.experimental.pallas import tpu as pltpu

F32 = jnp.float32
BF16 = jnp.bfloat16

D_MODEL = 1024
CHUNK = 64
A_HEADS = 8
A_QK = 64
A_V = 2 * A_QK
R_W = D_MODEL
R_BLOCKS = 8
R_BS = R_W // R_BLOCKS
CONV_W = 4
RG_C = 8.0
M_HEADS = 4
M_HD = 256
N_BRANCH = 3
IN_COLS = 11 * D_MODEL
EPS = 1e-5
NEG_INF = -1e30
SEC_Q, SEC_K, SEC_V, SEC_GA, SEC_XR, SEC_GB, SEC_QM, SEC_GC, SEC_GM = range(9)
BF_SECTIONS = (SEC_Q, SEC_QM)
QCOL_Q, QCOL_QM = 0, 1
F_SECTIONS = (SEC_GA, SEC_XR, SEC_GB, SEC_GC)
COL_GA, COL_XR, COL_GB, COL_GC, COL_GM = range(5)
KVCOL_K, KVCOL_V = 0, 1
HEAD_BLOCKS = D_MODEL // A_V

VMEM_LIMIT = 56 * 1024 * 1024
POS_SPLIT = 64


def _params(*sem):
    return pltpu.CompilerParams(dimension_semantics=sem, vmem_limit_bytes=VMEM_LIMIT)


def _nt_dot(a, b):
    return lax.dot_general(a, b, (((1,), (1,)), ((), ())), preferred_element_type=F32)


def _silu(x):
    return x * jax.nn.sigmoid(x)


def _layer_norm_rows(x, g, b):
    mu = jnp.mean(x, axis=-1, keepdims=True)
    xc = x - mu
    var = jnp.mean(xc * xc, axis=-1, keepdims=True)
    return xc * lax.rsqrt(var + EPS) * g + b


def _ln_kernel(x_ref, g_ref, b_ref, o_ref):
    o_ref[...] = _layer_norm_rows(x_ref[...], g_ref[...], b_ref[...])


def _input_ln(x, g, b):
    n = x.shape[0]
    tm = min(n, 1024)
    return pl.pallas_call(
        _ln_kernel,
        grid=(n // tm,),
        in_specs=[pl.BlockSpec((tm, D_MODEL), lambda i: (i, 0)),
                  pl.BlockSpec((1, D_MODEL), lambda i: (0, 0)),
                  pl.BlockSpec((1, D_MODEL), lambda i: (0, 0))],
        out_specs=pl.BlockSpec((tm, D_MODEL), lambda i: (i, 0)),
        out_shape=jax.ShapeDtypeStruct((n, D_MODEL), F32),
        compiler_params=_params("parallel"),
        name="input_ln",
    )(x, g.reshape(1, D_MODEL), b.reshape(1, D_MODEL))


def _proj_kernel(x_ref, w_ref, o_ref, xb_ref):
    @pl.when(pl.program_id(1) == 0)
    def _():
        xb_ref[...] = x_ref[...].astype(BF16)

    o_ref[...] = jnp.dot(xb_ref[...], w_ref[...], preferred_element_type=F32).astype(o_ref.dtype)


def _project(x, w_all, layer, out_dtype=F32):
    n = x.shape[0]
    cols = w_all.shape[2]
    tm = min(n, 1024)
    tn = D_MODEL
    return pl.pallas_call(
        _proj_kernel,
        grid=(n // tm, cols // tn),
        in_specs=[pl.BlockSpec((tm, D_MODEL), lambda i, j: (i, 0)),
                  pl.BlockSpec((None, D_MODEL, tn), lambda i, j: (layer, 0, j))],
        out_specs=pl.BlockSpec((tm, tn), lambda i, j: (i, j)),
        out_shape=jax.ShapeDtypeStruct((n, cols), out_dtype),
        scratch_shapes=[pltpu.VMEM((tm, D_MODEL), BF16)],
        compiler_params=_params("parallel", "arbitrary"),
        name="projection",
    )(x, w_all)


def _kv_proj_kernel(x_ref, w_ref, k5_in, v5_in, kb_ref, vb_ref, k5_ref, v5_ref):
    del k5_in, v5_in
    tm = x_ref.shape[0]
    xb = x_ref[...].astype(BF16)
    for col, b_ref, o5_ref in ((KVCOL_K, kb_ref, k5_ref), (KVCOL_V, vb_ref, v5_ref)):
        res = jnp.dot(xb, w_ref[:, col * D_MODEL:(col + 1) * D_MODEL], preferred_element_type=F32)
        b_ref[...] = res.astype(BF16)
        for h in range(A_HEADS):
            o5_ref[pl.ds(h, tm, stride=A_HEADS), :] = res[:, h * A_V:(h + 1) * A_V]


def _kv_project(x, w_all, layer, k5, v5):
    n = x.shape[0]
    tm = min(n, 512)
    any_spec = pl.BlockSpec(memory_space=pl.ANY)
    tok = pl.BlockSpec((tm, D_MODEL), lambda i: (i, 0))
    out5 = pl.BlockSpec((None, tm * A_HEADS, A_V), lambda i: (layer, i, 0))
    return pl.pallas_call(
        _kv_proj_kernel,
        grid=(n // tm,),
        in_specs=[tok, pl.BlockSpec((None, D_MODEL, 2 * D_MODEL), lambda i: (layer, 0, 0)), any_spec, any_spec],
        out_specs=[tok, tok, out5, out5],
        out_shape=[jax.ShapeDtypeStruct((n, D_MODEL), BF16), jax.ShapeDtypeStruct((n, D_MODEL), BF16),
                   jax.ShapeDtypeStruct(k5.shape, F32), jax.ShapeDtypeStruct(v5.shape, F32)],
        input_output_aliases={2: 2, 3: 3},
        compiler_params=_params("parallel"),
        name="kv_projection",
    )(x, w_all, k5, v5)


def _masked_halves(q):
    lane = lax.broadcasted_iota(jnp.int32, q.shape, 1)
    qs = q * (A_QK ** -0.5)
    zero = jnp.zeros_like(qs)
    return jnp.where(lane < A_QK, qs, zero), jnp.where(lane >= A_QK, qs, zero)


def _subln_gate(o, sg, lam_scale, g):
    o = o * lax.rsqrt(jnp.mean(o * o, axis=-1, keepdims=True) + EPS) * sg * lam_scale
    return o * _silu(g)


def _attn_prompt_kernel(scal_ref, q_ref, k_ref, v_ref, g_ref, sg_ref, o_ref,
                        kb_ref, vb_ref, corr_ref, qa_ref, sa_ref, sb_ref, m1_ref, a1_ref, m2_ref, a2_ref, *, tile):
    h = pl.program_id(1)
    lam = scal_ref[0]
    lam_scale = scal_ref[1]
    slope = scal_ref[2 + h]
    seq = k_ref.shape[0]
    rep = tile // A_V

    pos = lax.broadcasted_iota(jnp.int32, (seq, A_V), 0)
    lane = lax.broadcasted_iota(jnp.int32, (seq, A_V), 1)
    hi = ((pos // POS_SPLIT) * POS_SPLIT).astype(F32) * slope
    lo = (pos % POS_SPLIT).astype(F32) * slope
    kb_ref[:, :A_V] = k_ref[...]
    kb_ref[:, A_V:] = jnp.where(lane == 0, hi, jnp.where(lane == 1, lo, 0.0)).astype(BF16)
    vb_ref[:, :A_V] = v_ref[...]
    vb_ref[:, A_V:] = jnp.ones((seq, A_V), BF16)
    row = lax.broadcasted_iota(jnp.int32, (tile, tile), 0)
    col = lax.broadcasted_iota(jnp.int32, (tile, tile), 1)
    visible = (col // CHUNK) <= (row // CHUNK)
    corr_ref[...] = jnp.where(visible, jnp.minimum(0.0, (2.0 * slope) * (row - col).astype(F32)), NEG_INF)

    def scores(j, s_ref):
        kt = kb_ref[pl.ds(pl.multiple_of(j * tile, tile), tile), :]
        for half in range(2):
            s_ref[half] = _nt_dot(qa_ref[half], kt)

    def consume(j, s_ref, diagonal):
        vt = vb_ref[pl.ds(pl.multiple_of(j * tile, tile), tile), :]
        for half, (m_ref, a_ref) in enumerate(((m1_ref, a1_ref), (m2_ref, a2_ref))):
            s = s_ref[half]
            if diagonal:
                s = s + corr_ref[...]
            m_prev = m_ref[...]
            m_new = jnp.maximum(m_prev, jnp.max(s, axis=-1, keepdims=True))
            alpha = jnp.exp(m_prev - m_new)
            p = jnp.exp(s - jnp.concatenate([m_new] * rep, axis=-1))
            pv = jnp.dot(p.astype(BF16), vt, preferred_element_type=F32)
            a_ref[...] = jnp.concatenate([alpha, alpha], axis=-1) * a_ref[...] + pv
            m_ref[...] = m_new

    def pair(p, carry):
        j = 2 * p
        scores(j + 1, sb_ref)
        consume(j, sa_ref, False)
        scores(j + 2, sa_ref)
        consume(j + 1, sb_ref, False)
        return carry

    def query_tile(i, carry):
        rows = pl.ds(pl.multiple_of(i * tile, tile), tile)
        q_lo, q_hi = _masked_halves(q_ref[rows, :])
        lane_q = lax.broadcasted_iota(jnp.int32, (tile, A_V), 1)
        pos_cols = jnp.where(lane_q < 2, 1.0, 0.0).astype(BF16)
        qa_ref[0] = jnp.concatenate([q_lo, pos_cols], axis=-1)
        qa_ref[1] = jnp.concatenate([q_hi, pos_cols], axis=-1)
        for m_ref, a_ref in ((m1_ref, a1_ref), (m2_ref, a2_ref)):
            m_ref[...] = jnp.full(m_ref.shape, NEG_INF, F32)
            a_ref[...] = jnp.zeros(a_ref.shape, F32)

        scores(0, sa_ref)
        lax.fori_loop(0, i // 2, pair, 0)

        @pl.when(i % 2 == 1)
        def _():
            scores(i, sb_ref)
            consume(i - 1, sa_ref, False)
            consume(i, sb_ref, True)

        @pl.when(i % 2 == 0)
        def _():
            consume(i, sa_ref, True)

        o = (a1_ref[:, :A_V] / a1_ref[:, A_V:]) - lam * (a2_ref[:, :A_V] / a2_ref[:, A_V:])
        o_ref[rows, :] = _subln_gate(o, sg_ref[...], lam_scale, g_ref[rows, :]).astype(o_ref.dtype)
        return carry

    lax.fori_loop(0, seq // tile, query_tile, 0)


def _attn_prompt(pq, kb, vb, pf, scal, subln_g, batch, seq):
    tile = min(seq, 512)
    assert seq % tile == 0 and tile % CHUNK == 0 and tile % A_V == 0
    assert seq <= POS_SPLIT * 256
    kernel = functools.partial(_attn_prompt_kernel, tile=tile)
    hb = HEAD_BLOCKS
    return pl.pallas_call(
        kernel,
        grid=(batch, A_HEADS),
        in_specs=[pl.BlockSpec(memory_space=pltpu.SMEM),
                  pl.BlockSpec((None, seq, A_V), lambda b, h: (b, 0, QCOL_Q * hb + h)),
                  pl.BlockSpec((None, seq, A_V), lambda b, h: (b, 0, h)),
                  pl.BlockSpec((None, seq, A_V), lambda b, h: (b, 0, h)),
                  pl.BlockSpec((None, seq, A_V), lambda b, h: (b, 0, COL_GA * hb + h)),
                  pl.BlockSpec((1, A_V), lambda b, h: (0, 0))],
        out_specs=pl.BlockSpec((None, seq, A_V), lambda b, h: (b, 0, h)),
        out_shape=jax.ShapeDtypeStruct((batch, seq, D_MODEL), BF16),
        scratch_shapes=[pltpu.VMEM((seq, 2 * A_V), BF16), pltpu.VMEM((seq, 2 * A_V), BF16),
                        pltpu.VMEM((tile, tile), F32),
                        pltpu.VMEM((2, tile, 2 * A_V), BF16),
                        pltpu.VMEM((2, tile, tile), F32), pltpu.VMEM((2, tile, tile), F32),
                        pltpu.VMEM((tile, A_V), F32), pltpu.VMEM((tile, 2 * A_V), F32),
                        pltpu.VMEM((tile, A_V), F32), pltpu.VMEM((tile, 2 * A_V), F32)],
        compiler_params=_params("parallel", "parallel"),
        name="attn_prompt",
    )(scal, pq, kb, vb, pf, subln_g)


def _attn_sample_kernel(scal_ref, q_ref, kp_ref, vp_ref, kv_ref, g_ref, sg_ref, o_ref, *, past, steps):
    lam = scal_ref[0]
    lam_scale = scal_ref[1]

    def geometry(n_keys, first_key):
        pos_q = past + lax.broadcasted_iota(jnp.int32, (steps, n_keys), 0)
        pos_k = first_key + lax.broadcasted_iota(jnp.int32, (steps, n_keys), 1)
        return (pos_k // CHUNK) <= (pos_q // CHUNK), jnp.abs(pos_q - pos_k).astype(F32)

    vis_p, dist_p = geometry(past, 0)
    vis_n, dist_n = geometry(steps, past)

    for h in range(A_HEADS):
        slope = scal_ref[2 + h]
        cols = slice(h * A_V, (h + 1) * A_V)
        q_lo, q_hi = _masked_halves(q_ref[:, cols])
        kp = kp_ref[pl.ds(h, past, stride=A_HEADS), :].astype(BF16)
        vp = vp_ref[pl.ds(h, past, stride=A_HEADS), :].astype(BF16)
        kn = kv_ref[:, KVCOL_K * D_MODEL + h * A_V:KVCOL_K * D_MODEL + (h + 1) * A_V].astype(BF16)
        vn = kv_ref[:, KVCOL_V * D_MODEL + h * A_V:KVCOL_V * D_MODEL + (h + 1) * A_V].astype(BF16)
        bias_p = jnp.where(vis_p, -slope * dist_p, NEG_INF)
        bias_n = jnp.where(vis_n, -slope * dist_n, NEG_INF)

        def softmax(qq):
            s_p = _nt_dot(qq, kp) + bias_p
            s_n = _nt_dot(qq, kn) + bias_n
            m = jnp.maximum(jnp.max(s_p, axis=-1, keepdims=True), jnp.max(s_n, axis=-1, keepdims=True))
            e_p = jnp.exp(s_p - m)
            e_n = jnp.exp(s_n - m)
            l = jnp.sum(e_p, axis=-1, keepdims=True) + jnp.sum(e_n, axis=-1, keepdims=True)
            return e_p / l, e_n / l

        p1_p, p1_n = softmax(q_lo)
        p2_p, p2_n = softmax(q_hi)
        w_p = (p1_p - lam * p2_p).astype(BF16)
        w_n = (p1_n - lam * p2_n).astype(BF16)
        o = jnp.dot(w_p, vp, preferred_element_type=F32) + jnp.dot(w_n, vn, preferred_element_type=F32)
        o_ref[:, cols] = _subln_gate(o, sg_ref[...], lam_scale, g_ref[:, cols]).astype(o_ref.dtype)


def _attn_sample(pq, pkv, pf, cache_k, cache_v, layer, scal, subln_g, batch, steps):
    past = cache_k.shape[2] // A_HEADS
    kernel = functools.partial(_attn_sample_kernel, past=past, steps=steps)
    cache_spec = pl.BlockSpec((None, None, past * A_HEADS, A_V), lambda b: (layer, b, 0, 0))
    return pl.pallas_call(
        kernel,
        grid=(batch,),
        in_specs=[pl.BlockSpec(memory_space=pltpu.SMEM),
                  pl.BlockSpec((None, steps, D_MODEL), lambda b: (b, 0, QCOL_Q)),
                  cache_spec, cache_spec,
                  pl.BlockSpec((None, steps, 2 * D_MODEL), lambda b: (b, 0, 0)),
                  pl.BlockSpec((None, steps, D_MODEL), lambda b: (b, 0, COL_GA)),
                  pl.BlockSpec((1, A_V), lambda b: (0, 0))],
        out_specs=pl.BlockSpec((None, steps, D_MODEL), lambda b: (b, 0, 0)),
        out_shape=jax.ShapeDtypeStruct((batch, steps, D_MODEL), BF16),
        compiler_params=_params("parallel"),
        name="attn_sample",
    )(scal, pq, cache_k, cache_v, pkv, pf, subln_g)


CONV_PAD = 8


def _rglru_kernel(xr_ref, gb_ref, h0_ref, cbuf_ref, cw_ref, cb_ref, wa_ref, ba_ref, wx_ref, bx_ref, lam_ref,
                  ob_ref, hl_ref, xp_ref, a_ref, b_ref, h_ref, hc_ref, *, tt):
    t = pl.program_id(1)
    tail = CONV_W - 1

    @pl.when(t == 0)
    def _():
        xp_ref[CONV_PAD - tail:CONV_PAD, :] = cbuf_ref[...]
        hc_ref[...] = h0_ref[...]

    x = xr_ref[...]
    xp_ref[CONV_PAD:CONV_PAD + tt, :] = x
    xpad = xp_ref[...]

    def delayed(d):
        return pltpu.roll(xpad, d, axis=0)[CONV_PAD:CONV_PAD + tt, :]

    acc = delayed(3) * cw_ref[0:1, :]
    acc = acc + delayed(2) * cw_ref[1:2, :]
    acc = acc + delayed(1) * cw_ref[2:3, :]
    acc = acc + x * cw_ref[3:4, :]
    xc = cb_ref[...] + acc
    xp_ref[CONV_PAD - tail:CONV_PAD, :] = xp_ref[CONV_PAD + tt - tail:CONV_PAD + tt, :]

    xcb = xc.astype(BF16)

    def block_diag(w_ref):
        return jnp.concatenate(
            [jnp.dot(xcb[:, n * R_BS:(n + 1) * R_BS], w_ref[n], preferred_element_type=F32)
             for n in range(R_BLOCKS)], axis=-1)

    r = jax.nn.sigmoid(block_diag(wa_ref) + ba_ref[...])
    gate_i = jax.nn.sigmoid(block_diag(wx_ref) + bx_ref[...])
    neg_lam = -lam_ref[...]
    softplus = jnp.maximum(neg_lam, 0.0) + jnp.log1p(jnp.exp(-jnp.abs(neg_lam)))
    log_a = -RG_C * r * softplus
    a = jnp.exp(log_a)
    a_ref[...] = a
    b_ref[...] = jnp.sqrt(-jnp.tanh(log_a) * (a * a + 1.0)) * (gate_i * xc)

    def scan_row(s, h):
        h = a_ref[pl.ds(s, 1), :] * h + b_ref[pl.ds(s, 1), :]
        h_ref[pl.ds(s, 1), :] = h
        return h

    h_last = lax.fori_loop(0, tt, scan_row, hc_ref[...], unroll=8)
    hc_ref[...] = h_last
    hl_ref[...] = h_last
    ob_ref[...] = (h_ref[...] * _silu(gb_ref[...])).astype(ob_ref.dtype)


def _rglru(pf, h0, conv_buf, lw, batch, seq):
    tt = min(seq, 256)
    assert seq % tt == 0 and tt % 8 == 0
    kernel = functools.partial(_rglru_kernel, tt=tt)
    row = pl.BlockSpec((1, R_W), lambda b, t: (0, 0))
    wblk = pl.BlockSpec((R_BLOCKS, R_BS, R_BS), lambda b, t: (0, 0, 0))
    return pl.pallas_call(
        kernel,
        grid=(batch, seq // tt),
        in_specs=[pl.BlockSpec((None, tt, R_W), lambda b, t: (b, t, COL_XR)),
                  pl.BlockSpec((None, tt, R_W), lambda b, t: (b, t, COL_GB)),
                  pl.BlockSpec((None, 1, R_W), lambda b, t: (b, 0, 0)),
                  pl.BlockSpec((None, CONV_W - 1, R_W), lambda b, t: (b, 0, 0)),
                  pl.BlockSpec((CONV_W, R_W), lambda b, t: (0, 0)),
                  row, wblk, row, wblk, row, row],
        out_specs=[pl.BlockSpec((None, tt, R_W), lambda b, t: (b, t, 0)),
                   pl.BlockSpec((None, 1, R_W), lambda b, t: (b, 0, 0))],
        out_shape=[jax.ShapeDtypeStruct((batch, seq, R_W), BF16),
                   jax.ShapeDtypeStruct((batch, 1, R_W), F32)],
        scratch_shapes=[pltpu.VMEM((CONV_PAD + tt, R_W), F32),
                        pltpu.VMEM((tt, R_W), F32), pltpu.VMEM((tt, R_W), F32), pltpu.VMEM((tt, R_W), F32),
                        pltpu.VMEM((1, R_W), F32)],
        compiler_params=_params("parallel", "arbitrary"),
        name="conv_rglru",
    )(pf, pf, h0, conv_buf, lw["conv_w"], lw["conv_b"], lw["rg_wa"], lw["rg_ba"], lw["rg_wx"], lw["rg_bx"],
      lw["rg_lambda"])


def _cross_attention(qm, gc, mk_ref, mv_ref):
    parts = []
    for hd in range(M_HEADS):
        sl = slice(hd * M_HD, (hd + 1) * M_HD)
        qh = qm[:, sl] * (M_HD ** -0.5)
        s = _nt_dot(qh, mk_ref[:, sl].astype(BF16))
        e = jnp.exp(s - jnp.max(s, axis=-1, keepdims=True))
        p = (e / jnp.sum(e, axis=-1, keepdims=True)).astype(BF16)
        oh = jnp.dot(p, mv_ref[:, sl].astype(BF16), preferred_element_type=F32)
        parts.append(oh * _silu(gc[:, sl]))
    return jnp.concatenate(parts, axis=-1)


def _cross_kernel(qm_ref, gc_ref, mk_ref, mv_ref, oc_ref):
    oc_ref[...] = _cross_attention(qm_ref[...], gc_ref[...], mk_ref, mv_ref).astype(oc_ref.dtype)


def _cross_sample(pq, pf, mem_k, mem_v, layer, batch, steps):
    m_tok = mem_k.shape[2]
    return pl.pallas_call(
        _cross_kernel,
        grid=(batch,),
        in_specs=[pl.BlockSpec((None, steps, D_MODEL), lambda b: (b, 0, QCOL_QM)),
                  pl.BlockSpec((None, steps, D_MODEL), lambda b: (b, 0, COL_GC)),
                  pl.BlockSpec((None, None, m_tok, D_MODEL), lambda b: (layer, b, 0, 0)),
                  pl.BlockSpec((None, None, m_tok, D_MODEL), lambda b: (layer, b, 0, 0))],
        out_specs=pl.BlockSpec((None, steps, D_MODEL), lambda b: (b, 0, 0)),
        out_shape=jax.ShapeDtypeStruct((batch, steps, D_MODEL), BF16),
        compiler_params=_params("parallel"),
        name="cross_sample",
    )(pq, pf, mem_k, mem_v)


def _merge_kernel(*refs, alpha, fused_cross):
    if fused_cross:
        oa_ref, ob_ref, qm_ref, gc_ref, mk_ref, mv_ref, *refs = refs
        oc = _cross_attention(qm_ref[...], gc_ref[...], mk_ref, mv_ref)
    else:
        oa_ref, ob_ref, oc_ref, *refs = refs
        oc = oc_ref[...]
    *gm_refs, x_ref, wb_ref, wo_ref, lng_ref, lnb_ref, y_ref = refs
    m = None
    for n, o in enumerate((oa_ref[...], ob_ref[...], oc)):
        gate = jax.nn.sigmoid(gm_refs[n][...])
        term = gate * jnp.dot(o.astype(BF16), wb_ref[n], preferred_element_type=F32)
        m = term if m is None else m + term
    out = jnp.dot(m.astype(BF16), wo_ref[...], preferred_element_type=F32)
    y_ref[...] = _layer_norm_rows(alpha * x_ref[...] + out, lng_ref[...], lnb_ref[...])


def _merge(pq2d, pf2d, o_a, o_b, x, lw, alpha, *, o_c=None, mem_k=None, mem_v=None, seq=None):
    n = x.shape[0]
    fused = o_c is None
    tm = min(n, 512)
    tok = lambda c: pl.BlockSpec((tm, D_MODEL), lambda i: (i, c))
    full = lambda shape: pl.BlockSpec(shape, lambda i: (0,) * len(shape), pipeline_mode=pl.Buffered(1))
    in_specs = [tok(0), tok(0)]
    args = [o_a, o_b]
    if fused:
        assert seq % tm == 0
        per_batch = seq // tm
        m_tok = mem_k.shape[1]
        mem_spec = pl.BlockSpec((None, m_tok, D_MODEL), lambda i: (i // per_batch, 0, 0))
        in_specs += [tok(QCOL_QM), tok(COL_GC), mem_spec, mem_spec]
        args += [pq2d, pf2d, mem_k, mem_v]
    else:
        in_specs += [tok(0)]
        args += [o_c]
    in_specs += [tok(COL_GM + n) for n in range(N_BRANCH)]
    in_specs += [tok(0),
                 full((N_BRANCH, D_MODEL, D_MODEL)), full((D_MODEL, D_MODEL)),
                 full((1, D_MODEL)), full((1, D_MODEL))]
    args += [pf2d] * N_BRANCH + [x, lw["w_branch"], lw["w_o"], lw["ln_g"], lw["ln_b"]]
    kernel = functools.partial(_merge_kernel, alpha=alpha, fused_cross=fused)
    return pl.pallas_call(
        kernel,
        grid=(n // tm,),
        in_specs=in_specs,
        out_specs=pl.BlockSpec((tm, D_MODEL), lambda i: (i, 0)),
        out_shape=jax.ShapeDtypeStruct((n, D_MODEL), F32),
        compiler_params=_params("parallel"),
        name="merge_fused" if fused else "merge",
    )(*args)


def _sections(w, secs):
    return jnp.concatenate([w[:, :, s * D_MODEL:(s + 1) * D_MODEL] for s in secs], axis=-1).astype(BF16)


def kernel(x_prompt, x_sample, cache_k, cache_v, cache_mem_k, cache_mem_v, state_rnn_h, state_conv, mem_prompt,
           ln_in_g, ln_in_b, w_in, lambda_q1, lambda_k1, lambda_q2, lambda_k2, subln_g, conv_w, conv_b, rg_wa,
           rg_ba, rg_wx, rg_bx, rg_lambda, w_mem_kv, w_branch, w_o, ln_g, ln_b):
    bp, sp, _ = x_prompt.shape
    bs, ss, _ = x_sample.shape
    depth = w_in.shape[0]
    past = cache_k.shape[2]
    m_tok = mem_prompt.shape[1]
    alpha = (2 * depth) ** 0.25
    assert sp >= CONV_W - 1 and ss >= CONV_W - 1
    assert w_in.shape[2] == IN_COLS

    w_q_b = _sections(w_in, BF_SECTIONS)
    w_kv_b = _sections(w_in, (SEC_K, SEC_V))
    w_f_b = jnp.concatenate([_sections(w_in, F_SECTIONS), w_in[:, :, SEC_GM * D_MODEL:].astype(BF16)], axis=-1)
    w_mem_b = w_mem_kv.astype(BF16)
    w_branch_b = w_branch.astype(BF16)
    w_o_b = w_o.astype(BF16)
    rg_wa_b = rg_wa.astype(BF16)
    rg_wx_b = rg_wx.astype(BF16)
    cache_k2 = cache_k.reshape(depth, bs, past * A_HEADS, A_V)
    cache_v2 = cache_v.reshape(depth, bs, past * A_HEADS, A_V)
    cmem_k2 = cache_mem_k.reshape(depth, bs, m_tok, D_MODEL)
    cmem_v2 = cache_mem_v.reshape(depth, bs, m_tok, D_MODEL)
    slopes = jnp.exp2(-8.0 * jnp.arange(1, A_HEADS + 1, dtype=F32) / A_HEADS)

    xp = _input_ln(x_prompt.reshape(bp * sp, D_MODEL), ln_in_g, ln_in_b)
    xs = _input_ln(x_sample.reshape(bs * ss, D_MODEL), ln_in_g, ln_in_b)
    mem2d = mem_prompt.reshape(bp * m_tok, D_MODEL)
    h0_p = jnp.zeros((bp, 1, R_W), F32)
    buf0_p = jnp.zeros((bp, CONV_W - 1, R_W), F32)
    nq = len(BF_SECTIONS) * D_MODEL
    nf = (len(F_SECTIONS) + N_BRANCH) * D_MODEL
    k5 = jnp.zeros((depth, bp * sp * A_HEADS, A_V), F32)
    v5 = jnp.zeros((depth, bp * sp * A_HEADS, A_V), F32)

    def sec(p3, col):
        return p3[:, :, col * D_MODEL:(col + 1) * D_MODEL]

    pmk, pmv, ph, pc = [], [], [], []
    sk, sv, sh, sc = [], [], [], []
    for l in range(depth):
        lam_init = 0.8 - 0.6 * math.exp(-0.3 * l)
        lam = (jnp.exp(jnp.sum(lambda_q1[l] * lambda_k1[l])) - jnp.exp(jnp.sum(lambda_q2[l] * lambda_k2[l]))
               + lam_init)
        scal = jnp.concatenate([jnp.stack([lam, jnp.asarray(1.0 - lam_init, F32)]), slopes]).astype(F32)
        sg = subln_g[l].reshape(1, A_V)
        lw = dict(conv_w=conv_w[l], conv_b=conv_b[l].reshape(1, R_W), rg_wa=rg_wa_b[l],
                  rg_ba=rg_ba[l].reshape(1, R_W), rg_wx=rg_wx_b[l], rg_bx=rg_bx[l].reshape(1, R_W),
                  rg_lambda=rg_lambda[l].reshape(1, R_W), w_branch=w_branch_b[l], w_o=w_o_b[l],
                  ln_g=ln_g[l].reshape(1, D_MODEL), ln_b=ln_b[l].reshape(1, D_MODEL))

        mkv = _project(mem2d, w_mem_b, l)
        mk = mkv[:, :D_MODEL].reshape(bp, m_tok, D_MODEL)
        mv = mkv[:, D_MODEL:].reshape(bp, m_tok, D_MODEL)
        pq = _project(xp, w_q_b, l, BF16)
        pf = _project(xp, w_f_b, l)
        kb, vb, k5, v5 = _kv_project(xp, w_kv_b, l, k5, v5)
        pq3 = pq.reshape(bp, sp, nq)
        pf3 = pf.reshape(bp, sp, nf)
        o_a = _attn_prompt(pq3, kb.reshape(bp, sp, D_MODEL), vb.reshape(bp, sp, D_MODEL), pf3, scal, sg, bp, sp)
        o_b, h_new = _rglru(pf3, h0_p, buf0_p, lw, bp, sp)
        xp = _merge(pq, pf, o_a.reshape(bp * sp, D_MODEL), o_b.reshape(bp * sp, D_MODEL), xp, lw, alpha,
                    mem_k=mk, mem_v=mv, seq=sp)
        pmk.append(mk.reshape(bp, m_tok, M_HEADS, M_HD))
        pmv.append(mv.reshape(bp, m_tok, M_HEADS, M_HD))
        ph.append(h_new.reshape(bp, R_W))
        pc.append(sec(pf3, COL_XR)[:, sp - (CONV_W - 1):])

        pq = _project(xs, w_q_b, l, BF16)
        pf = _project(xs, w_f_b, l)
        pkv3 = _project(xs, w_kv_b, l).reshape(bs, ss, 2 * D_MODEL)
        pq3 = pq.reshape(bs, ss, nq)
        pf3 = pf.reshape(bs, ss, nf)
        o_a = _attn_sample(pq3, pkv3, pf3, cache_k2, cache_v2, l, scal, sg, bs, ss)
        o_b, h_new = _rglru(pf3, state_rnn_h[l].reshape(bs, 1, R_W), state_conv[l], lw, bs, ss)
        o_c = _cross_sample(pq3, pf3, cmem_k2, cmem_v2, l, bs, ss)
        xs = _merge(pq, pf, o_a.reshape(bs * ss, D_MODEL), o_b.reshape(bs * ss, D_MODEL), xs, lw, alpha,
                    o_c=o_c.reshape(bs * ss, D_MODEL))
        sk.append(sec(pkv3, KVCOL_K).reshape(bs, ss, A_HEADS, 2 * A_QK))
        sv.append(sec(pkv3, KVCOL_V).reshape(bs, ss, A_HEADS, A_V))
        sh.append(h_new.reshape(bs, R_W))
        sc.append(sec(pf3, COL_XR)[:, ss - (CONV_W - 1):])

    return (xp.reshape(bp, sp, D_MODEL), xs.reshape(bs, ss, D_MODEL),
            k5.reshape(depth, bp, sp, A_HEADS, 2 * A_QK), v5.reshape(depth, bp, sp, A_HEADS, A_V), jnp.stack(pmk), jnp.stack(pmv), jnp.stack(ph), jnp.stack(pc),
            jnp.stack(sk), jnp.stack(sv), jnp.stack(sh), jnp.stack(sc))
```

```python
import functools
import math

import jax
import jax.numpy as jnp
from jax import lax
from jax.experimental import pallas as pl
from jax.experimental.pallas import tpu as pltpu

F32 = jnp.float32
BF16 = jnp.bfloat16

D_MODEL = 1024
CHUNK = 64
A_HEADS = 8
A_QK = 64
A_V = 2 * A_QK
R_W = D_MODEL
R_BLOCKS = 8
R_BS = R_W // R_BLOCKS
CONV_W = 4
RG_C = 8.0
M_HEADS = 4
M_HD = 256
N_BRANCH = 3
IN_COLS = 11 * D_MODEL
EPS = 1e-5
NEG_INF = -1e30
SEC_Q, SEC_K, SEC_V, SEC_GA, SEC_XR, SEC_GB, SEC_QM, SEC_GC, SEC_GM = range(9)
BF_SECTIONS = (SEC_Q, SEC_QM)
QCOL_Q, QCOL_QM = 0, 1
F_SECTIONS = (SEC_GA, SEC_XR, SEC_GB, SEC_GC)
COL_GA, COL_XR, COL_GB, COL_GC, COL_GM = range(5)
KVCOL_K, KVCOL_V = 0, 1
HEAD_BLOCKS = D_MODEL // A_V

VMEM_LIMIT = 56 * 1024 * 1024
PROJ_MAX_COLS = 2048
POS_SPLIT = 64


def _params(*sem):
    return pltpu.CompilerParams(dimension_semantics=sem, vmem_limit_bytes=VMEM_LIMIT)


def _nt_dot(a, b):
    return lax.dot_general(a, b, (((1,), (1,)), ((), ())), preferred_element_type=F32)


def _silu(x):
    return x * jax.nn.sigmoid(x)


def _layer_norm_rows(x, g, b):
    mu = jnp.mean(x, axis=-1, keepdims=True)
    xc = x - mu
    var = jnp.mean(xc * xc, axis=-1, keepdims=True)
    return xc * lax.rsqrt(var + EPS) * g + b


def _ln_kernel(x_ref, g_ref, b_ref, o_ref):
    o_ref[...] = _layer_norm_rows(x_ref[...], g_ref[...], b_ref[...])


def _input_ln(x, g, b):
    n = x.shape[0]
    tm = min(n, 1024)
    return pl.pallas_call(
        _ln_kernel,
        grid=(n // tm,),
        in_specs=[pl.BlockSpec((tm, D_MODEL), lambda i: (i, 0)),
                  pl.BlockSpec((1, D_MODEL), lambda i: (0, 0)),
                  pl.BlockSpec((1, D_MODEL), lambda i: (0, 0))],
        out_specs=pl.BlockSpec((tm, D_MODEL), lambda i: (i, 0)),
        out_shape=jax.ShapeDtypeStruct((n, D_MODEL), F32),
        compiler_params=_params("parallel"),
        name="input_ln",
    )(x, g.reshape(1, D_MODEL), b.reshape(1, D_MODEL))


def _proj_kernel(x_ref, w_ref, o_ref, xb_ref):
    @pl.when(pl.program_id(1) == 0)
    def _():
        xb_ref[...] = x_ref[...].astype(BF16)

    o_ref[...] = jnp.dot(xb_ref[...], w_ref[...], preferred_element_type=F32).astype(o_ref.dtype)


def _project(x, w_all, layer, out_dtype=F32):
    n = x.shape[0]
    cols = w_all.shape[2]
    tm = min(n, 1024)
    tn = max(t for t in range(A_V, PROJ_MAX_COLS + 1, A_V) if cols % t == 0)
    return pl.pallas_call(
        _proj_kernel,
        grid=(n // tm, cols // tn),
        in_specs=[pl.BlockSpec((tm, D_MODEL), lambda i, j: (i, 0)),
                  pl.BlockSpec((None, D_MODEL, tn), lambda i, j: (layer, 0, j))],
        out_specs=pl.BlockSpec((tm, tn), lambda i, j: (i, j)),
        out_shape=jax.ShapeDtypeStruct((n, cols), out_dtype),
        scratch_shapes=[pltpu.VMEM((tm, D_MODEL), BF16)],
        compiler_params=_params("parallel", "arbitrary"),
        name="projection",
    )(x, w_all)


def _kv_proj_kernel(x_ref, w_ref, k5_in, v5_in, kb_ref, vb_ref, k5_ref, v5_ref):
    del k5_in, v5_in
    tm = x_ref.shape[0]
    xb = x_ref[...].astype(BF16)
    for col, b_ref, o5_ref in ((KVCOL_K, kb_ref, k5_ref), (KVCOL_V, vb_ref, v5_ref)):
        res = jnp.dot(xb, w_ref[:, col * D_MODEL:(col + 1) * D_MODEL], preferred_element_type=F32)
        b_ref[...] = res.astype(BF16)
        for h in range(A_HEADS):
            o5_ref[pl.ds(h, tm, stride=A_HEADS), :] = res[:, h * A_V:(h + 1) * A_V]


def _kv_project(x, w_all, layer, k5, v5):
    n = x.shape[0]
    tm = min(n, 512)
    any_spec = pl.BlockSpec(memory_space=pl.ANY)
    tok = pl.BlockSpec((tm, D_MODEL), lambda i: (i, 0))
    out5 = pl.BlockSpec((None, tm * A_HEADS, A_V), lambda i: (layer, i, 0))
    return pl.pallas_call(
        _kv_proj_kernel,
        grid=(n // tm,),
        in_specs=[tok, pl.BlockSpec((None, D_MODEL, 2 * D_MODEL), lambda i: (layer, 0, 0)), any_spec, any_spec],
        out_specs=[tok, tok, out5, out5],
        out_shape=[jax.ShapeDtypeStruct((n, D_MODEL), BF16), jax.ShapeDtypeStruct((n, D_MODEL), BF16),
                   jax.ShapeDtypeStruct(k5.shape, F32), jax.ShapeDtypeStruct(v5.shape, F32)],
        input_output_aliases={2: 2, 3: 3},
        compiler_params=_params("parallel"),
        name="kv_projection",
    )(x, w_all, k5, v5)


def _masked_halves(q):
    lane = lax.broadcasted_iota(jnp.int32, q.shape, 1)
    qs = q * (A_QK ** -0.5)
    zero = jnp.zeros_like(qs)
    return jnp.where(lane < A_QK, qs, zero), jnp.where(lane >= A_QK, qs, zero)


def _subln_gate(o, sg, lam_scale, g):
    o = o * lax.rsqrt(jnp.mean(o * o, axis=-1, keepdims=True) + EPS) * sg * lam_scale
    return o * _silu(g)


def _attn_prompt_kernel(scal_ref, q_ref, k_ref, v_ref, g_ref, sg_ref, o_ref,
                        kb_ref, vb_ref, corr_ref, qa_ref, sa_ref, sb_ref, sd_ref, m1_ref, a1_ref, m2_ref, a2_ref,
                        *, tile):
    h = pl.program_id(1)
    lam = scal_ref[0]
    lam_scale = scal_ref[1]
    slope = scal_ref[2 + h]
    seq = k_ref.shape[0]

    pos = lax.broadcasted_iota(jnp.int32, (seq, A_V), 0)
    lane = lax.broadcasted_iota(jnp.int32, (seq, A_V), 1)
    hi = ((pos // POS_SPLIT) * POS_SPLIT).astype(F32) * slope
    lo = (pos % POS_SPLIT).astype(F32) * slope
    kb_ref[:, :A_V] = k_ref[...]
    kb_ref[:, A_V:] = jnp.where(lane == 0, hi, jnp.where(lane == 1, lo, 0.0)).astype(BF16)
    vb_ref[:, :A_V] = v_ref[...]
    vb_ref[:, A_V:] = jnp.ones((seq, A_V), BF16)
    row = lax.broadcasted_iota(jnp.int32, (tile, tile), 0)
    col = lax.broadcasted_iota(jnp.int32, (tile, tile), 1)
    visible = (col // CHUNK) <= (row // CHUNK)
    corr_ref[...] = jnp.where(visible, jnp.minimum(0.0, (2.0 * slope) * (row - col).astype(F32)), NEG_INF)

    stats = ((m1_ref, a1_ref), (m2_ref, a2_ref))
    top = slice(0, tile // 2)
    bottom = slice(tile // 2, tile)

    def online_softmax(s, rows, m_ref, a_ref, vt):
        m_prev = m_ref[rows, :]
        m_new = jnp.maximum(m_prev, jnp.max(s, axis=-1, keepdims=True))
        alpha = jnp.exp(m_prev - m_new)
        p = jnp.exp(s - jnp.concatenate([m_new] * (s.shape[1] // A_V), axis=-1))
        pv = jnp.dot(p.astype(BF16), vt, preferred_element_type=F32)
        a_ref[rows, :] = jnp.concatenate([alpha, alpha], axis=-1) * a_ref[rows, :] + pv
        m_ref[rows, :] = m_new

    def scores(j, s_ref):
        kt = kb_ref[pl.ds(pl.multiple_of(j * tile, tile), tile), :]
        for half in range(2):
            s_ref[half] = _nt_dot(qa_ref[half], kt)

    def consume(j, s_ref):
        vt = vb_ref[pl.ds(pl.multiple_of(j * tile, tile), tile), :]
        for half, (m_ref, a_ref) in enumerate(stats):
            online_softmax(s_ref[half], slice(None), m_ref, a_ref, vt)

    def scores_diagonal(i, s_ref):
        off = pl.multiple_of(i * tile, tile)
        for half in range(2):
            s_ref[half, top, :tile // 2] = _nt_dot(qa_ref[half, top, :], kb_ref[pl.ds(off, tile // 2), :])
            s_ref[half, bottom, :] = _nt_dot(qa_ref[half, bottom, :], kb_ref[pl.ds(off, tile), :])

    def consume_diagonal(i, s_ref):
        off = pl.multiple_of(i * tile, tile)
        for half, (m_ref, a_ref) in enumerate(stats):
            online_softmax(s_ref[half, top, :tile // 2] + corr_ref[top, :tile // 2], top, m_ref, a_ref,
                           vb_ref[pl.ds(off, tile // 2), :])
            online_softmax(s_ref[half, bottom, :] + corr_ref[bottom, :], bottom, m_ref, a_ref,
                           vb_ref[pl.ds(off, tile), :])

    def pair(p, carry):
        j = 2 * p
        scores(j + 1, sb_ref)
        consume(j, sa_ref)
        scores(j + 2, sa_ref)
        consume(j + 1, sb_ref)
        return carry

    def query_tile(i, carry):
        rows = pl.ds(pl.multiple_of(i * tile, tile), tile)
        q_lo, q_hi = _masked_halves(q_ref[rows, :])
        lane_q = lax.broadcasted_iota(jnp.int32, (tile, A_V), 1)
        pos_cols = jnp.where(lane_q < 2, 1.0, 0.0).astype(BF16)
        qa_ref[0] = jnp.concatenate([q_lo, pos_cols], axis=-1)
        qa_ref[1] = jnp.concatenate([q_hi, pos_cols], axis=-1)
        for m_ref, a_ref in ((m1_ref, a1_ref), (m2_ref, a2_ref)):
            m_ref[...] = jnp.full(m_ref.shape, NEG_INF, F32)
            a_ref[...] = jnp.zeros(a_ref.shape, F32)

        scores_diagonal(i, sd_ref)
        scores(0, sa_ref)
        consume_diagonal(i, sd_ref)
        full_pairs = jnp.maximum(i - 1, 0) // 2
        lax.fori_loop(0, full_pairs, pair, 0)
        j0 = 2 * full_pairs
        left = i - j0

        @pl.when(left == 1)
        def _():
            consume(j0, sa_ref)

        @pl.when(left == 2)
        def _():
            scores(j0 + 1, sb_ref)
            consume(j0, sa_ref)
            consume(j0 + 1, sb_ref)

        o = (a1_ref[:, :A_V] / a1_ref[:, A_V:]) - lam * (a2_ref[:, :A_V] / a2_ref[:, A_V:])
        o_ref[rows, :] = _subln_gate(o, sg_ref[...], lam_scale, g_ref[rows, :]).astype(o_ref.dtype)
        return carry

    lax.fori_loop(0, seq // tile, query_tile, 0)


def _attn_prompt(pq, kb, vb, pf, scal, subln_g, batch, seq):
    tile = min(seq, 512)
    assert seq % tile == 0 and (tile // 2) % CHUNK == 0 and (tile // 2) % A_V == 0
    assert seq <= POS_SPLIT * 256
    kernel = functools.partial(_attn_prompt_kernel, tile=tile)
    hb = HEAD_BLOCKS
    return pl.pallas_call(
        kernel,
        grid=(batch, A_HEADS),
        in_specs=[pl.BlockSpec(memory_space=pltpu.SMEM),
                  pl.BlockSpec((None, seq, A_V), lambda b, h: (b, 0, QCOL_Q * hb + h)),
                  pl.BlockSpec((None, seq, A_V), lambda b, h: (b, 0, h)),
                  pl.BlockSpec((None, seq, A_V), lambda b, h: (b, 0, h)),
                  pl.BlockSpec((None, seq, A_V), lambda b, h: (b, 0, COL_GA * hb + h)),
                  pl.BlockSpec((1, A_V), lambda b, h: (0, 0))],
        out_specs=pl.BlockSpec((None, seq, A_V), lambda b, h: (b, 0, h)),
        out_shape=jax.ShapeDtypeStruct((batch, seq, D_MODEL), BF16),
        scratch_shapes=[pltpu.VMEM((seq, 2 * A_V), BF16), pltpu.VMEM((seq, 2 * A_V), BF16),
                        pltpu.VMEM((tile, tile), F32),
                        pltpu.VMEM((2, tile, 2 * A_V), BF16),
                        pltpu.VMEM((2, tile, tile), F32), pltpu.VMEM((2, tile, tile), F32),
                        pltpu.VMEM((2, tile, tile), F32),
                        pltpu.VMEM((tile, A_V), F32), pltpu.VMEM((tile, 2 * A_V), F32),
                        pltpu.VMEM((tile, A_V), F32), pltpu.VMEM((tile, 2 * A_V), F32)],
        compiler_params=_params("parallel", "parallel"),
        name="attn_prompt",
    )(scal, pq, kb, vb, pf, subln_g)


def _attn_sample_kernel(scal_ref, q_ref, kp_ref, vp_ref, kv_ref, g_ref, sg_ref, o_ref, *, past, steps):
    lam = scal_ref[0]
    lam_scale = scal_ref[1]

    def geometry(n_keys, first_key):
        pos_q = past + lax.broadcasted_iota(jnp.int32, (steps, n_keys), 0)
        pos_k = first_key + lax.broadcasted_iota(jnp.int32, (steps, n_keys), 1)
        return (pos_k // CHUNK) <= (pos_q // CHUNK), jnp.abs(pos_q - pos_k).astype(F32)

    vis_p, dist_p = geometry(past, 0)
    vis_n, dist_n = geometry(steps, past)

    for h in range(A_HEADS):
        slope = scal_ref[2 + h]
        cols = slice(h * A_V, (h + 1) * A_V)
        q_lo, q_hi = _masked_halves(q_ref[:, cols])
        kp = kp_ref[pl.ds(h, past, stride=A_HEADS), :].astype(BF16)
        vp = vp_ref[pl.ds(h, past, stride=A_HEADS), :].astype(BF16)
        kn = kv_ref[:, KVCOL_K * D_MODEL + h * A_V:KVCOL_K * D_MODEL + (h + 1) * A_V].astype(BF16)
        vn = kv_ref[:, KVCOL_V * D_MODEL + h * A_V:KVCOL_V * D_MODEL + (h + 1) * A_V].astype(BF16)
        bias_p = jnp.where(vis_p, -slope * dist_p, NEG_INF)
        bias_n = jnp.where(vis_n, -slope * dist_n, NEG_INF)

        def softmax(qq):
            s_p = _nt_dot(qq, kp) + bias_p
            s_n = _nt_dot(qq, kn) + bias_n
            m = jnp.maximum(jnp.max(s_p, axis=-1, keepdims=True), jnp.max(s_n, axis=-1, keepdims=True))
            e_p = jnp.exp(s_p - m)
            e_n = jnp.exp(s_n - m)
            l = jnp.sum(e_p, axis=-1, keepdims=True) + jnp.sum(e_n, axis=-1, keepdims=True)
            return e_p / l, e_n / l

        p1_p, p1_n = softmax(q_lo)
        p2_p, p2_n = softmax(q_hi)
        w_p = (p1_p - lam * p2_p).astype(BF16)
        w_n = (p1_n - lam * p2_n).astype(BF16)
        o = jnp.dot(w_p, vp, preferred_element_type=F32) + jnp.dot(w_n, vn, preferred_element_type=F32)
        o_ref[:, cols] = _subln_gate(o, sg_ref[...], lam_scale, g_ref[:, cols]).astype(o_ref.dtype)


def _attn_sample(pq, pkv, pf, cache_k, cache_v, layer, scal, subln_g, batch, steps):
    past = cache_k.shape[2] // A_HEADS
    kernel = functools.partial(_attn_sample_kernel, past=past, steps=steps)
    cache_spec = pl.BlockSpec((None, None, past * A_HEADS, A_V), lambda b: (layer, b, 0, 0))
    return pl.pallas_call(
        kernel,
        grid=(batch,),
        in_specs=[pl.BlockSpec(memory_space=pltpu.SMEM),
                  pl.BlockSpec((None, steps, D_MODEL), lambda b: (b, 0, QCOL_Q)),
                  cache_spec, cache_spec,
                  pl.BlockSpec((None, steps, 2 * D_MODEL), lambda b: (b, 0, 0)),
                  pl.BlockSpec((None, steps, D_MODEL), lambda b: (b, 0, COL_GA)),
                  pl.BlockSpec((1, A_V), lambda b: (0, 0))],
        out_specs=pl.BlockSpec((None, steps, D_MODEL), lambda b: (b, 0, 0)),
        out_shape=jax.ShapeDtypeStruct((batch, steps, D_MODEL), BF16),
        compiler_params=_params("parallel"),
        name="attn_sample",
    )(scal, pq, cache_k, cache_v, pkv, pf, subln_g)


CONV_PAD = 8


def _rglru_kernel(xr_ref, gb_ref, h0_ref, cbuf_ref, cw_ref, cb_ref, wa_ref, ba_ref, wx_ref, bx_ref, lam_ref,
                  ob_ref, hl_ref, xp_ref, a_ref, b_ref, h_ref, hc_ref, *, tt):
    t = pl.program_id(1)
    tail = CONV_W - 1

    @pl.when(t == 0)
    def _():
        xp_ref[CONV_PAD - tail:CONV_PAD, :] = cbuf_ref[...]
        hc_ref[...] = h0_ref[...]

    x = xr_ref[...]
    xp_ref[CONV_PAD:CONV_PAD + tt, :] = x
    xpad = xp_ref[...]

    def delayed(d):
        return pltpu.roll(xpad, d, axis=0)[CONV_PAD:CONV_PAD + tt, :]

    acc = delayed(3) * cw_ref[0:1, :]
    acc = acc + delayed(2) * cw_ref[1:2, :]
    acc = acc + delayed(1) * cw_ref[2:3, :]
    acc = acc + x * cw_ref[3:4, :]
    xc = cb_ref[...] + acc
    xp_ref[CONV_PAD - tail:CONV_PAD, :] = xp_ref[CONV_PAD + tt - tail:CONV_PAD + tt, :]

    xcb = xc.astype(BF16)

    def block_diag(w_ref):
        return jnp.concatenate(
            [jnp.dot(xcb[:, n * R_BS:(n + 1) * R_BS], w_ref[n], preferred_element_type=F32)
             for n in range(R_BLOCKS)], axis=-1)

    r = jax.nn.sigmoid(block_diag(wa_ref) + ba_ref[...])
    gate_i = jax.nn.sigmoid(block_diag(wx_ref) + bx_ref[...])
    neg_lam = -lam_ref[...]
    softplus = jnp.maximum(neg_lam, 0.0) + jnp.log1p(jnp.exp(-jnp.abs(neg_lam)))
    log_a = -RG_C * r * softplus
    a = jnp.exp(log_a)
    a_ref[...] = a
    b_ref[...] = jnp.sqrt(-jnp.tanh(log_a) * (a * a + 1.0)) * (gate_i * xc)

    def scan_row(s, h):
        h = a_ref[pl.ds(s, 1), :] * h + b_ref[pl.ds(s, 1), :]
        h_ref[pl.ds(s, 1), :] = h
        return h

    h_last = lax.fori_loop(0, tt, scan_row, hc_ref[...], unroll=8)
    hc_ref[...] = h_last
    hl_ref[...] = h_last
    ob_ref[...] = (h_ref[...] * _silu(gb_ref[...])).astype(ob_ref.dtype)


def _rglru(pf, h0, conv_buf, lw, batch, seq):
    tt = min(seq, 256)
    assert seq % tt == 0 and tt % 8 == 0
    kernel = functools.partial(_rglru_kernel, tt=tt)
    row = pl.BlockSpec((1, R_W), lambda b, t: (0, 0))
    wblk = pl.BlockSpec((R_BLOCKS, R_BS, R_BS), lambda b, t: (0, 0, 0))
    return pl.pallas_call(
        kernel,
        grid=(batch, seq // tt),
        in_specs=[pl.BlockSpec((None, tt, R_W), lambda b, t: (b, t, COL_XR)),
                  pl.BlockSpec((None, tt, R_W), lambda b, t: (b, t, COL_GB)),
                  pl.BlockSpec((None, 1, R_W), lambda b, t: (b, 0, 0)),
                  pl.BlockSpec((None, CONV_W - 1, R_W), lambda b, t: (b, 0, 0)),
                  pl.BlockSpec((CONV_W, R_W), lambda b, t: (0, 0)),
                  row, wblk, row, wblk, row, row],
        out_specs=[pl.BlockSpec((None, tt, R_W), lambda b, t: (b, t, 0)),
                   pl.BlockSpec((None, 1, R_W), lambda b, t: (b, 0, 0))],
        out_shape=[jax.ShapeDtypeStruct((batch, seq, R_W), BF16),
                   jax.ShapeDtypeStruct((batch, 1, R_W), F32)],
        scratch_shapes=[pltpu.VMEM((CONV_PAD + tt, R_W), F32),
                        pltpu.VMEM((tt, R_W), F32), pltpu.VMEM((tt, R_W), F32), pltpu.VMEM((tt, R_W), F32),
                        pltpu.VMEM((1, R_W), F32)],
        compiler_params=_params("parallel", "arbitrary"),
        name="conv_rglru",
    )(pf, pf, h0, conv_buf, lw["conv_w"], lw["conv_b"], lw["rg_wa"], lw["rg_ba"], lw["rg_wx"], lw["rg_bx"],
      lw["rg_lambda"])


def _cross_attention(qm, gc, mk_ref, mv_ref):
    parts = []
    for hd in range(M_HEADS):
        sl = slice(hd * M_HD, (hd + 1) * M_HD)
        qh = qm[:, sl] * (M_HD ** -0.5)
        s = _nt_dot(qh, mk_ref[:, sl].astype(BF16))
        e = jnp.exp(s - jnp.max(s, axis=-1, keepdims=True))
        p = (e / jnp.sum(e, axis=-1, keepdims=True)).astype(BF16)
        oh = jnp.dot(p, mv_ref[:, sl].astype(BF16), preferred_element_type=F32)
        parts.append(oh * _silu(gc[:, sl]))
    return jnp.concatenate(parts, axis=-1)


def _cross_kernel(qm_ref, gc_ref, mk_ref, mv_ref, oc_ref):
    oc_ref[...] = _cross_attention(qm_ref[...], gc_ref[...], mk_ref, mv_ref).astype(oc_ref.dtype)


def _cross_sample(pq, pf, mem_k, mem_v, layer, batch, steps):
    m_tok = mem_k.shape[2]
    return pl.pallas_call(
        _cross_kernel,
        grid=(batch,),
        in_specs=[pl.BlockSpec((None, steps, D_MODEL), lambda b: (b, 0, QCOL_QM)),
                  pl.BlockSpec((None, steps, D_MODEL), lambda b: (b, 0, COL_GC)),
                  pl.BlockSpec((None, None, m_tok, D_MODEL), lambda b: (layer, b, 0, 0)),
                  pl.BlockSpec((None, None, m_tok, D_MODEL), lambda b: (layer, b, 0, 0))],
        out_specs=pl.BlockSpec((None, steps, D_MODEL), lambda b: (b, 0, 0)),
        out_shape=jax.ShapeDtypeStruct((batch, steps, D_MODEL), BF16),
        compiler_params=_params("parallel"),
        name="cross_sample",
    )(pq, pf, mem_k, mem_v)


def _merge_kernel(*refs, alpha, fused_cross):
    if fused_cross:
        oa_ref, ob_ref, qm_ref, gc_ref, mk_ref, mv_ref, *refs = refs
        oc = _cross_attention(qm_ref[...], gc_ref[...], mk_ref, mv_ref)
    else:
        oa_ref, ob_ref, oc_ref, *refs = refs
        oc = oc_ref[...]
    *gm_refs, x_ref, wb_ref, wo_ref, lng_ref, lnb_ref, y_ref = refs
    m = None
    for n, o in enumerate((oa_ref[...], ob_ref[...], oc)):
        gate = jax.nn.sigmoid(gm_refs[n][...])
        term = gate * jnp.dot(o.astype(BF16), wb_ref[n], preferred_element_type=F32)
        m = term if m is None else m + term
    out = jnp.dot(m.astype(BF16), wo_ref[...], preferred_element_type=F32)
    y_ref[...] = _layer_norm_rows(alpha * x_ref[...] + out, lng_ref[...], lnb_ref[...])


def _merge(pq2d, pf2d, o_a, o_b, x, lw, alpha, *, o_c=None, mem_k=None, mem_v=None, seq=None):
    n = x.shape[0]
    fused = o_c is None
    tm = min(n, 512)
    tok = lambda c: pl.BlockSpec((tm, D_MODEL), lambda i: (i, c))
    full = lambda shape: pl.BlockSpec(shape, lambda i: (0,) * len(shape), pipeline_mode=pl.Buffered(1))
    in_specs = [tok(0), tok(0)]
    args = [o_a, o_b]
    if fused:
        assert seq % tm == 0
        per_batch = seq // tm
        m_tok = mem_k.shape[1]
        mem_spec = pl.BlockSpec((None, m_tok, D_MODEL), lambda i: (i // per_batch, 0, 0))
        in_specs += [tok(QCOL_QM), tok(COL_GC), mem_spec, mem_spec]
        args += [pq2d, pf2d, mem_k, mem_v]
    else:
        in_specs += [tok(0)]
        args += [o_c]
    in_specs += [tok(COL_GM + n) for n in range(N_BRANCH)]
    in_specs += [tok(0),
                 full((N_BRANCH, D_MODEL, D_MODEL)), full((D_MODEL, D_MODEL)),
                 full((1, D_MODEL)), full((1, D_MODEL))]
    args += [pf2d] * N_BRANCH + [x, lw["w_branch"], lw["w_o"], lw["ln_g"], lw["ln_b"]]
    kernel = functools.partial(_merge_kernel, alpha=alpha, fused_cross=fused)
    return pl.pallas_call(
        kernel,
        grid=(n // tm,),
        in_specs=in_specs,
        out_specs=pl.BlockSpec((tm, D_MODEL), lambda i: (i, 0)),
        out_shape=jax.ShapeDtypeStruct((n, D_MODEL), F32),
        compiler_params=_params("parallel"),
        name="merge_fused" if fused else "merge",
    )(*args)


def _sections(w, secs):
    return jnp.concatenate([w[:, :, s * D_MODEL:(s + 1) * D_MODEL] for s in secs], axis=-1).astype(BF16)


def kernel(x_prompt, x_sample, cache_k, cache_v, cache_mem_k, cache_mem_v, state_rnn_h, state_conv, mem_prompt,
           ln_in_g, ln_in_b, w_in, lambda_q1, lambda_k1, lambda_q2, lambda_k2, subln_g, conv_w, conv_b, rg_wa,
           rg_ba, rg_wx, rg_bx, rg_lambda, w_mem_kv, w_branch, w_o, ln_g, ln_b):
    bp, sp, _ = x_prompt.shape
    bs, ss, _ = x_sample.shape
    depth = w_in.shape[0]
    past = cache_k.shape[2]
    m_tok = mem_prompt.shape[1]
    alpha = (2 * depth) ** 0.25
    assert sp >= CONV_W - 1 and ss >= CONV_W - 1
    assert w_in.shape[2] == IN_COLS

    w_q_b = _sections(w_in, BF_SECTIONS)
    w_kv_b = _sections(w_in, (SEC_K, SEC_V))
    w_f_b = jnp.concatenate([_sections(w_in, F_SECTIONS), w_in[:, :, SEC_GM * D_MODEL:].astype(BF16)], axis=-1)
    w_mem_b = w_mem_kv.astype(BF16)
    w_branch_b = w_branch.astype(BF16)
    w_o_b = w_o.astype(BF16)
    rg_wa_b = rg_wa.astype(BF16)
    rg_wx_b = rg_wx.astype(BF16)
    cache_k2 = cache_k.reshape(depth, bs, past * A_HEADS, A_V)
    cache_v2 = cache_v.reshape(depth, bs, past * A_HEADS, A_V)
    cmem_k2 = cache_mem_k.reshape(depth, bs, m_tok, D_MODEL)
    cmem_v2 = cache_mem_v.reshape(depth, bs, m_tok, D_MODEL)
    slopes = jnp.exp2(-8.0 * jnp.arange(1, A_HEADS + 1, dtype=F32) / A_HEADS)

    xp = _input_ln(x_prompt.reshape(bp * sp, D_MODEL), ln_in_g, ln_in_b)
    xs = _input_ln(x_sample.reshape(bs * ss, D_MODEL), ln_in_g, ln_in_b)
    mem2d = mem_prompt.reshape(bp * m_tok, D_MODEL)
    h0_p = jnp.zeros((bp, 1, R_W), F32)
    buf0_p = jnp.zeros((bp, CONV_W - 1, R_W), F32)
    nq = len(BF_SECTIONS) * D_MODEL
    nf = (len(F_SECTIONS) + N_BRANCH) * D_MODEL
    k5 = jnp.zeros((depth, bp * sp * A_HEADS, A_V), F32)
    v5 = jnp.zeros((depth, bp * sp * A_HEADS, A_V), F32)

    def sec(p3, col):
        return p3[:, :, col * D_MODEL:(col + 1) * D_MODEL]

    pmk, pmv, ph, pc = [], [], [], []
    sk, sv, sh, sc = [], [], [], []
    for l in range(depth):
        lam_init = 0.8 - 0.6 * math.exp(-0.3 * l)
        lam = (jnp.exp(jnp.sum(lambda_q1[l] * lambda_k1[l])) - jnp.exp(jnp.sum(lambda_q2[l] * lambda_k2[l]))
               + lam_init)
        scal = jnp.concatenate([jnp.stack([lam, jnp.asarray(1.0 - lam_init, F32)]), slopes]).astype(F32)
        sg = subln_g[l].reshape(1, A_V)
        lw = dict(conv_w=conv_w[l], conv_b=conv_b[l].reshape(1, R_W), rg_wa=rg_wa_b[l],
                  rg_ba=rg_ba[l].reshape(1, R_W), rg_wx=rg_wx_b[l], rg_bx=rg_bx[l].reshape(1, R_W),
                  rg_lambda=rg_lambda[l].reshape(1, R_W), w_branch=w_branch_b[l], w_o=w_o_b[l],
                  ln_g=ln_g[l].reshape(1, D_MODEL), ln_b=ln_b[l].reshape(1, D_MODEL))

        mkv = _project(mem2d, w_mem_b, l)
        mk = mkv[:, :D_MODEL].reshape(bp, m_tok, D_MODEL)
        mv = mkv[:, D_MODEL:].reshape(bp, m_tok, D_MODEL)
        pq = _project(xp, w_q_b, l, BF16)
        pf = _project(xp, w_f_b, l)
        kb, vb, k5, v5 = _kv_project(xp, w_kv_b, l, k5, v5)
        pq3 = pq.reshape(bp, sp, nq)
        pf3 = pf.reshape(bp, sp, nf)
        o_a = _attn_prompt(pq3, kb.reshape(bp, sp, D_MODEL), vb.reshape(bp, sp, D_MODEL), pf3, scal, sg, bp, sp)
        o_b, h_new = _rglru(pf3, h0_p, buf0_p, lw, bp, sp)
        xp = _merge(pq, pf, o_a.reshape(bp * sp, D_MODEL), o_b.reshape(bp * sp, D_MODEL), xp, lw, alpha,
                    mem_k=mk, mem_v=mv, seq=sp)
        pmk.append(mk.reshape(bp, m_tok, M_HEADS, M_HD))
        pmv.append(mv.reshape(bp, m_tok, M_HEADS, M_HD))
        ph.append(h_new.reshape(bp, R_W))
        pc.append(sec(pf3, COL_XR)[:, sp - (CONV_W - 1):])

        pq = _project(xs, w_q_b, l, BF16)
        pf = _project(xs, w_f_b, l)
        pkv3 = _project(xs, w_kv_b, l).reshape(bs, ss, 2 * D_MODEL)
        pq3 = pq.reshape(bs, ss, nq)
        pf3 = pf.reshape(bs, ss, nf)
        o_a = _attn_sample(pq3, pkv3, pf3, cache_k2, cache_v2, l, scal, sg, bs, ss)
        o_b, h_new = _rglru(pf3, state_rnn_h[l].reshape(bs, 1, R_W), state_conv[l], lw, bs, ss)
        o_c = _cross_sample(pq3, pf3, cmem_k2, cmem_v2, l, bs, ss)
        xs = _merge(pq, pf, o_a.reshape(bs * ss, D_MODEL), o_b.reshape(bs * ss, D_MODEL), xs, lw, alpha,
                    o_c=o_c.reshape(bs * ss, D_MODEL))
        sk.append(sec(pkv3, KVCOL_K).reshape(bs, ss, A_HEADS, 2 * A_QK))
        sv.append(sec(pkv3, KVCOL_V).reshape(bs, ss, A_HEADS, A_V))
        sh.append(h_new.reshape(bs, R_W))
        sc.append(sec(pf3, COL_XR)[:, ss - (CONV_W - 1):])

    return (xp.reshape(bp, sp, D_MODEL), xs.reshape(bs, ss, D_MODEL),
            k5.reshape(depth, bp, sp, A_HEADS, 2 * A_QK), v5.reshape(depth, bp, sp, A_HEADS, A_V), jnp.stack(pmk), jnp.stack(pmv), jnp.stack(ph), jnp.stack(pc),
            jnp.stack(sk), jnp.stack(sv), jnp.stack(sh), jnp.stack(sc))
```

```python
import functools
import math

import jax
import jax.numpy as jnp
from jax import lax
from jax.experimental import pallas as pl
from jax.experimental.pallas import tpu as pltpu

F32 = jnp.float32
BF16 = jnp.bfloat16

D_MODEL = 1024
CHUNK = 64
A_HEADS = 8
A_QK = 64
A_V = 2 * A_QK
R_W = D_MODEL
R_BLOCKS = 8
R_BS = R_W // R_BLOCKS
CONV_W = 4
RG_C = 8.0
M_HEADS = 4
M_HD = 256
N_BRANCH = 3
IN_COLS = 11 * D_MODEL
EPS = 1e-5
NEG_INF = -1e30
SEC_Q, SEC_K, SEC_V, SEC_GA, SEC_XR, SEC_GB, SEC_QM, SEC_GC, SEC_GM = range(9)
BF_SECTIONS = (SEC_Q, SEC_QM)
QCOL_Q, QCOL_QM = 0, 1
F_SECTIONS = (SEC_GA, SEC_XR, SEC_GB, SEC_GC)
COL_GA, COL_XR, COL_GB, COL_GC, COL_GM = range(5)
KVCOL_K, KVCOL_V = 0, 1
HEAD_BLOCKS = D_MODEL // A_V

VMEM_LIMIT = 56 * 1024 * 1024
PROJ_MAX_COLS = 2048
POS_SPLIT = 64


def _params(*sem):
    return pltpu.CompilerParams(dimension_semantics=sem, vmem_limit_bytes=VMEM_LIMIT)


def _nt_dot(a, b):
    return lax.dot_general(a, b, (((1,), (1,)), ((), ())), preferred_element_type=F32)


def _silu(x):
    return x * jax.nn.sigmoid(x)


def _layer_norm_rows(x, g, b):
    mu = jnp.mean(x, axis=-1, keepdims=True)
    xc = x - mu
    var = jnp.mean(xc * xc, axis=-1, keepdims=True)
    return xc * lax.rsqrt(var + EPS) * g + b


def _ln_kernel(x_ref, g_ref, b_ref, o_ref):
    o_ref[...] = _layer_norm_rows(x_ref[...], g_ref[...], b_ref[...])


def _input_ln(x, g, b):
    n = x.shape[0]
    tm = min(n, 1024)
    return pl.pallas_call(
        _ln_kernel,
        grid=(n // tm,),
        in_specs=[pl.BlockSpec((tm, D_MODEL), lambda i: (i, 0)),
                  pl.BlockSpec((1, D_MODEL), lambda i: (0, 0)),
                  pl.BlockSpec((1, D_MODEL), lambda i: (0, 0))],
        out_specs=pl.BlockSpec((tm, D_MODEL), lambda i: (i, 0)),
        out_shape=jax.ShapeDtypeStruct((n, D_MODEL), F32),
        compiler_params=_params("parallel"),
        name="input_ln",
    )(x, g.reshape(1, D_MODEL), b.reshape(1, D_MODEL))


def _proj_kernel(x_ref, w_ref, o_ref, xb_ref):
    @pl.when(pl.program_id(1) == 0)
    def _():
        xb_ref[...] = x_ref[...].astype(BF16)

    o_ref[...] = jnp.dot(xb_ref[...], w_ref[...], preferred_element_type=F32).astype(o_ref.dtype)


def _project(x, w_all, layer, out_dtype=F32):
    n = x.shape[0]
    cols = w_all.shape[2]
    tm = min(n, 1024)
    tn = max(t for t in range(A_V, PROJ_MAX_COLS + 1, A_V) if cols % t == 0)
    return pl.pallas_call(
        _proj_kernel,
        grid=(n // tm, cols // tn),
        in_specs=[pl.BlockSpec((tm, D_MODEL), lambda i, j: (i, 0)),
                  pl.BlockSpec((None, D_MODEL, tn), lambda i, j: (layer, 0, j))],
        out_specs=pl.BlockSpec((tm, tn), lambda i, j: (i, j)),
        out_shape=jax.ShapeDtypeStruct((n, cols), out_dtype),
        scratch_shapes=[pltpu.VMEM((tm, D_MODEL), BF16)],
        compiler_params=_params("parallel", "arbitrary"),
        name="projection",
    )(x, w_all)


def _kv_proj_kernel(x_ref, w_ref, k5_in, v5_in, kb_ref, vb_ref, k5_ref, v5_ref):
    del k5_in, v5_in
    tm = x_ref.shape[0]
    xb = x_ref[...].astype(BF16)
    for col, b_ref, o5_ref in ((KVCOL_K, kb_ref, k5_ref), (KVCOL_V, vb_ref, v5_ref)):
        res = jnp.dot(xb, w_ref[:, col * D_MODEL:(col + 1) * D_MODEL], preferred_element_type=F32)
        b_ref[...] = res.astype(BF16)
        for h in range(A_HEADS):
            o5_ref[pl.ds(h, tm, stride=A_HEADS), :] = res[:, h * A_V:(h + 1) * A_V]


def _kv_project(x, w_all, layer, k5, v5):
    n = x.shape[0]
    tm = min(n, 512)
    any_spec = pl.BlockSpec(memory_space=pl.ANY)
    tok = pl.BlockSpec((tm, D_MODEL), lambda i: (i, 0))
    out5 = pl.BlockSpec((None, tm * A_HEADS, A_V), lambda i: (layer, i, 0))
    return pl.pallas_call(
        _kv_proj_kernel,
        grid=(n // tm,),
        in_specs=[tok, pl.BlockSpec((None, D_MODEL, 2 * D_MODEL), lambda i: (layer, 0, 0)), any_spec, any_spec],
        out_specs=[tok, tok, out5, out5],
        out_shape=[jax.ShapeDtypeStruct((n, D_MODEL), BF16), jax.ShapeDtypeStruct((n, D_MODEL), BF16),
                   jax.ShapeDtypeStruct(k5.shape, F32), jax.ShapeDtypeStruct(v5.shape, F32)],
        input_output_aliases={2: 2, 3: 3},
        compiler_params=_params("parallel"),
        name="kv_projection",
    )(x, w_all, k5, v5)


def _masked_halves(q):
    lane = lax.broadcasted_iota(jnp.int32, q.shape, 1)
    qs = q * (A_QK ** -0.5)
    zero = jnp.zeros_like(qs)
    return jnp.where(lane < A_QK, qs, zero), jnp.where(lane >= A_QK, qs, zero)


def _subln_gate(o, sg, lam_scale, g):
    o = o * lax.rsqrt(jnp.mean(o * o, axis=-1, keepdims=True) + EPS) * sg * lam_scale
    return o * _silu(g)


def _attn_prompt_kernel(scal_ref, q_ref, k_ref, v_ref, g_ref, sg_ref, o_ref,
                        kb_ref, vb_ref, corr_ref, qa_ref, sa_ref, sb_ref, sd_ref, m1_ref, a1_ref, m2_ref, a2_ref,
                        *, tile):
    h = pl.program_id(1)
    lam = scal_ref[0]
    lam_scale = scal_ref[1]
    slope = scal_ref[2 + h]
    seq = k_ref.shape[0]

    pos = lax.broadcasted_iota(jnp.int32, (seq, A_V), 0)
    lane = lax.broadcasted_iota(jnp.int32, (seq, A_V), 1)
    hi = ((pos // POS_SPLIT) * POS_SPLIT).astype(F32) * slope
    lo = (pos % POS_SPLIT).astype(F32) * slope
    kb_ref[:, :A_V] = k_ref[...]
    kb_ref[:, A_V:] = jnp.where(lane == 0, hi, jnp.where(lane == 1, lo, 0.0)).astype(BF16)
    vb_ref[:, :A_V] = v_ref[...]
    vb_ref[:, A_V:] = jnp.ones((seq, A_V), BF16)
    row = lax.broadcasted_iota(jnp.int32, (tile, tile), 0)
    col = lax.broadcasted_iota(jnp.int32, (tile, tile), 1)
    visible = (col // CHUNK) <= (row // CHUNK)
    corr_ref[...] = jnp.where(visible, jnp.minimum(0.0, (2.0 * slope) * (row - col).astype(F32)), NEG_INF)

    stats = ((m1_ref, a1_ref), (m2_ref, a2_ref))
    top = slice(0, tile // 2)
    bottom = slice(tile // 2, tile)

    def online_softmax(s, rows, m_ref, a_ref, vt):
        m_prev = m_ref[rows, :]
        m_new = jnp.maximum(m_prev, jnp.max(s, axis=-1, keepdims=True))
        alpha = jnp.exp(m_prev - m_new)
        p = jnp.exp(s - jnp.concatenate([m_new] * (s.shape[1] // A_V), axis=-1))
        pv = jnp.dot(p.astype(BF16), vt, preferred_element_type=F32)
        a_ref[rows, :] = jnp.concatenate([alpha, alpha], axis=-1) * a_ref[rows, :] + pv
        m_ref[rows, :] = m_new

    def scores(j, s_ref):
        kt = kb_ref[pl.ds(pl.multiple_of(j * tile, tile), tile), :]
        for half in range(2):
            s_ref[half] = _nt_dot(qa_ref[half], kt)

    def consume(j, s_ref):
        vt = vb_ref[pl.ds(pl.multiple_of(j * tile, tile), tile), :]
        for half, (m_ref, a_ref) in enumerate(stats):
            online_softmax(s_ref[half], slice(None), m_ref, a_ref, vt)

    def scores_diagonal(i, s_ref):
        off = pl.multiple_of(i * tile, tile)
        for half in range(2):
            s_ref[half, top, :tile // 2] = _nt_dot(qa_ref[half, top, :], kb_ref[pl.ds(off, tile // 2), :])
            s_ref[half, bottom, :] = _nt_dot(qa_ref[half, bottom, :], kb_ref[pl.ds(off, tile), :])

    def consume_diagonal(i, s_ref):
        off = pl.multiple_of(i * tile, tile)
        for half, (m_ref, a_ref) in enumerate(stats):
            online_softmax(s_ref[half, top, :tile // 2] + corr_ref[top, :tile // 2], top, m_ref, a_ref,
                           vb_ref[pl.ds(off, tile // 2), :])
            online_softmax(s_ref[half, bottom, :] + corr_ref[bottom, :], bottom, m_ref, a_ref,
                           vb_ref[pl.ds(off, tile), :])

    def pair(p, carry):
        j = 2 * p
        scores(j + 1, sb_ref)
        consume(j, sa_ref)
        scores(j + 2, sa_ref)
        consume(j + 1, sb_ref)
        return carry

    def query_tile(i, carry):
        rows = pl.ds(pl.multiple_of(i * tile, tile), tile)
        q_lo, q_hi = _masked_halves(q_ref[rows, :])
        lane_q = lax.broadcasted_iota(jnp.int32, (tile, A_V), 1)
        pos_cols = jnp.where(lane_q < 2, 1.0, 0.0).astype(BF16)
        qa_ref[0] = jnp.concatenate([q_lo, pos_cols], axis=-1)
        qa_ref[1] = jnp.concatenate([q_hi, pos_cols], axis=-1)
        for m_ref, a_ref in ((m1_ref, a1_ref), (m2_ref, a2_ref)):
            m_ref[...] = jnp.full(m_ref.shape, NEG_INF, F32)
            a_ref[...] = jnp.zeros(a_ref.shape, F32)

        scores_diagonal(i, sd_ref)
        scores(0, sa_ref)
        consume_diagonal(i, sd_ref)
        full_pairs = jnp.maximum(i - 1, 0) // 2

        def quad(p, carry):
            pair(2 * p, carry)
            return pair(2 * p + 1, carry)

        lax.fori_loop(0, full_pairs // 2, quad, 0)
        lax.fori_loop(2 * (full_pairs // 2), full_pairs, pair, 0)
        j0 = 2 * full_pairs
        left = i - j0

        @pl.when(left == 1)
        def _():
            consume(j0, sa_ref)

        @pl.when(left == 2)
        def _():
            scores(j0 + 1, sb_ref)
            consume(j0, sa_ref)
            consume(j0 + 1, sb_ref)

        o = (a1_ref[:, :A_V] / a1_ref[:, A_V:]) - lam * (a2_ref[:, :A_V] / a2_ref[:, A_V:])
        o_ref[rows, :] = _subln_gate(o, sg_ref[...], lam_scale, g_ref[rows, :]).astype(o_ref.dtype)
        return carry

    lax.fori_loop(0, seq // tile, query_tile, 0)


def _attn_prompt(pq, kb, vb, pf, scal, subln_g, batch, seq):
    tile = min(seq, 512)
    assert seq % tile == 0 and (tile // 2) % CHUNK == 0 and (tile // 2) % A_V == 0
    assert seq <= POS_SPLIT * 256
    kernel = functools.partial(_attn_prompt_kernel, tile=tile)
    hb = HEAD_BLOCKS
    return pl.pallas_call(
        kernel,
        grid=(batch, A_HEADS),
        in_specs=[pl.BlockSpec(memory_space=pltpu.SMEM),
                  pl.BlockSpec((None, seq, A_V), lambda b, h: (b, 0, QCOL_Q * hb + h)),
                  pl.BlockSpec((None, seq, A_V), lambda b, h: (b, 0, h)),
                  pl.BlockSpec((None, seq, A_V), lambda b, h: (b, 0, h)),
                  pl.BlockSpec((None, seq, A_V), lambda b, h: (b, 0, COL_GA * hb + h)),
                  pl.BlockSpec((1, A_V), lambda b, h: (0, 0))],
        out_specs=pl.BlockSpec((None, seq, A_V), lambda b, h: (b, 0, h)),
        out_shape=jax.ShapeDtypeStruct((batch, seq, D_MODEL), BF16),
        scratch_shapes=[pltpu.VMEM((seq, 2 * A_V), BF16), pltpu.VMEM((seq, 2 * A_V), BF16),
                        pltpu.VMEM((tile, tile), F32),
                        pltpu.VMEM((2, tile, 2 * A_V), BF16),
                        pltpu.VMEM((2, tile, tile), F32), pltpu.VMEM((2, tile, tile), F32),
                        pltpu.VMEM((2, tile, tile), F32),
                        pltpu.VMEM((tile, A_V), F32), pltpu.VMEM((tile, 2 * A_V), F32),
                        pltpu.VMEM((tile, A_V), F32), pltpu.VMEM((tile, 2 * A_V), F32)],
        compiler_params=_params("parallel", "parallel"),
        name="attn_prompt",
    )(scal, pq, kb, vb, pf, subln_g)


def _attn_sample_kernel(scal_ref, q_ref, kp_ref, vp_ref, kv_ref, g_ref, sg_ref, o_ref, *, past, steps):
    lam = scal_ref[0]
    lam_scale = scal_ref[1]

    def geometry(n_keys, first_key):
        pos_q = past + lax.broadcasted_iota(jnp.int32, (steps, n_keys), 0)
        pos_k = first_key + lax.broadcasted_iota(jnp.int32, (steps, n_keys), 1)
        return (pos_k // CHUNK) <= (pos_q // CHUNK), jnp.abs(pos_q - pos_k).astype(F32)

    vis_p, dist_p = geometry(past, 0)
    vis_n, dist_n = geometry(steps, past)

    for h in range(A_HEADS):
        slope = scal_ref[2 + h]
        cols = slice(h * A_V, (h + 1) * A_V)
        q_lo, q_hi = _masked_halves(q_ref[:, cols])
        kp = kp_ref[pl.ds(h, past, stride=A_HEADS), :].astype(BF16)
        vp = vp_ref[pl.ds(h, past, stride=A_HEADS), :].astype(BF16)
        kn = kv_ref[:, KVCOL_K * D_MODEL + h * A_V:KVCOL_K * D_MODEL + (h + 1) * A_V].astype(BF16)
        vn = kv_ref[:, KVCOL_V * D_MODEL + h * A_V:KVCOL_V * D_MODEL + (h + 1) * A_V].astype(BF16)
        bias_p = jnp.where(vis_p, -slope * dist_p, NEG_INF)
        bias_n = jnp.where(vis_n, -slope * dist_n, NEG_INF)

        def softmax(qq):
            s_p = _nt_dot(qq, kp) + bias_p
            s_n = _nt_dot(qq, kn) + bias_n
            m = jnp.maximum(jnp.max(s_p, axis=-1, keepdims=True), jnp.max(s_n, axis=-1, keepdims=True))
            e_p = jnp.exp(s_p - m)
            e_n = jnp.exp(s_n - m)
            l = jnp.sum(e_p, axis=-1, keepdims=True) + jnp.sum(e_n, axis=-1, keepdims=True)
            return e_p / l, e_n / l

        p1_p, p1_n = softmax(q_lo)
        p2_p, p2_n = softmax(q_hi)
        w_p = (p1_p - lam * p2_p).astype(BF16)
        w_n = (p1_n - lam * p2_n).astype(BF16)
        o = jnp.dot(w_p, vp, preferred_element_type=F32) + jnp.dot(w_n, vn, preferred_element_type=F32)
        o_ref[:, cols] = _subln_gate(o, sg_ref[...], lam_scale, g_ref[:, cols]).astype(o_ref.dtype)


def _attn_sample(pq, pkv, pf, cache_k, cache_v, layer, scal, subln_g, batch, steps):
    past = cache_k.shape[2] // A_HEADS
    kernel = functools.partial(_attn_sample_kernel, past=past, steps=steps)
    cache_spec = pl.BlockSpec((None, None, past * A_HEADS, A_V), lambda b: (layer, b, 0, 0))
    return pl.pallas_call(
        kernel,
        grid=(batch,),
        in_specs=[pl.BlockSpec(memory_space=pltpu.SMEM),
                  pl.BlockSpec((None, steps, D_MODEL), lambda b: (b, 0, QCOL_Q)),
                  cache_spec, cache_spec,
                  pl.BlockSpec((None, steps, 2 * D_MODEL), lambda b: (b, 0, 0)),
                  pl.BlockSpec((None, steps, D_MODEL), lambda b: (b, 0, COL_GA)),
                  pl.BlockSpec((1, A_V), lambda b: (0, 0))],
        out_specs=pl.BlockSpec((None, steps, D_MODEL), lambda b: (b, 0, 0)),
        out_shape=jax.ShapeDtypeStruct((batch, steps, D_MODEL), BF16),
        compiler_params=_params("parallel"),
        name="attn_sample",
    )(scal, pq, cache_k, cache_v, pkv, pf, subln_g)


CONV_PAD = 8


def _rglru_kernel(xr_ref, gb_ref, h0_ref, cbuf_ref, cw_ref, cb_ref, wa_ref, ba_ref, wx_ref, bx_ref, lam_ref,
                  ob_ref, hl_ref, xp_ref, a_ref, b_ref, h_ref, hc_ref, *, tt):
    t = pl.program_id(1)
    tail = CONV_W - 1

    @pl.when(t == 0)
    def _():
        xp_ref[CONV_PAD - tail:CONV_PAD, :] = cbuf_ref[...]
        hc_ref[...] = h0_ref[...]

    x = xr_ref[...]
    xp_ref[CONV_PAD:CONV_PAD + tt, :] = x
    xpad = xp_ref[...]

    def delayed(d):
        return pltpu.roll(xpad, d, axis=0)[CONV_PAD:CONV_PAD + tt, :]

    acc = delayed(3) * cw_ref[0:1, :]
    acc = acc + delayed(2) * cw_ref[1:2, :]
    acc = acc + delayed(1) * cw_ref[2:3, :]
    acc = acc + x * cw_ref[3:4, :]
    xc = cb_ref[...] + acc
    xp_ref[CONV_PAD - tail:CONV_PAD, :] = xp_ref[CONV_PAD + tt - tail:CONV_PAD + tt, :]

    xcb = xc.astype(BF16)

    def block_diag(w_ref):
        return jnp.concatenate(
            [jnp.dot(xcb[:, n * R_BS:(n + 1) * R_BS], w_ref[n], preferred_element_type=F32)
             for n in range(R_BLOCKS)], axis=-1)

    r = jax.nn.sigmoid(block_diag(wa_ref) + ba_ref[...])
    gate_i = jax.nn.sigmoid(block_diag(wx_ref) + bx_ref[...])
    neg_lam = -lam_ref[...]
    softplus = jnp.maximum(neg_lam, 0.0) + jnp.log1p(jnp.exp(-jnp.abs(neg_lam)))
    log_a = -RG_C * r * softplus
    a = jnp.exp(log_a)
    a_ref[...] = a
    b_ref[...] = jnp.sqrt(-jnp.tanh(log_a) * (a * a + 1.0)) * (gate_i * xc)

    def scan_row(s, h):
        h = a_ref[pl.ds(s, 1), :] * h + b_ref[pl.ds(s, 1), :]
        h_ref[pl.ds(s, 1), :] = h
        return h

    h_last = lax.fori_loop(0, tt, scan_row, hc_ref[...], unroll=True)
    hc_ref[...] = h_last
    hl_ref[...] = h_last
    ob_ref[...] = (h_ref[...] * _silu(gb_ref[...])).astype(ob_ref.dtype)


def _rglru(pf, h0, conv_buf, lw, batch, seq):
    tt = min(seq, 256)
    assert seq % tt == 0 and tt % 8 == 0
    kernel = functools.partial(_rglru_kernel, tt=tt)
    row = pl.BlockSpec((1, R_W), lambda b, t: (0, 0))
    wblk = pl.BlockSpec((R_BLOCKS, R_BS, R_BS), lambda b, t: (0, 0, 0))
    return pl.pallas_call(
        kernel,
        grid=(batch, seq // tt),
        in_specs=[pl.BlockSpec((None, tt, R_W), lambda b, t: (b, t, COL_XR)),
                  pl.BlockSpec((None, tt, R_W), lambda b, t: (b, t, COL_GB)),
                  pl.BlockSpec((None, 1, R_W), lambda b, t: (b, 0, 0)),
                  pl.BlockSpec((None, CONV_W - 1, R_W), lambda b, t: (b, 0, 0)),
                  pl.BlockSpec((CONV_W, R_W), lambda b, t: (0, 0)),
                  row, wblk, row, wblk, row, row],
        out_specs=[pl.BlockSpec((None, tt, R_W), lambda b, t: (b, t, 0)),
                   pl.BlockSpec((None, 1, R_W), lambda b, t: (b, 0, 0))],
        out_shape=[jax.ShapeDtypeStruct((batch, seq, R_W), BF16),
                   jax.ShapeDtypeStruct((batch, 1, R_W), F32)],
        scratch_shapes=[pltpu.VMEM((CONV_PAD + tt, R_W), F32),
                        pltpu.VMEM((tt, R_W), F32), pltpu.VMEM((tt, R_W), F32), pltpu.VMEM((tt, R_W), F32),
                        pltpu.VMEM((1, R_W), F32)],
        compiler_params=_params("parallel", "arbitrary"),
        name="conv_rglru",
    )(pf, pf, h0, conv_buf, lw["conv_w"], lw["conv_b"], lw["rg_wa"], lw["rg_ba"], lw["rg_wx"], lw["rg_bx"],
      lw["rg_lambda"])


def _cross_attention(qm, gc, mk_ref, mv_ref):
    parts = []
    for hd in range(M_HEADS):
        sl = slice(hd * M_HD, (hd + 1) * M_HD)
        qh = qm[:, sl] * (M_HD ** -0.5)
        s = _nt_dot(qh, mk_ref[:, sl].astype(BF16))
        e = jnp.exp(s - jnp.max(s, axis=-1, keepdims=True))
        p = (e / jnp.sum(e, axis=-1, keepdims=True)).astype(BF16)
        oh = jnp.dot(p, mv_ref[:, sl].astype(BF16), preferred_element_type=F32)
        parts.append(oh * _silu(gc[:, sl]))
    return jnp.concatenate(parts, axis=-1)


def _cross_kernel(qm_ref, gc_ref, mk_ref, mv_ref, oc_ref):
    oc_ref[...] = _cross_attention(qm_ref[...], gc_ref[...], mk_ref, mv_ref).astype(oc_ref.dtype)


def _cross_sample(pq, pf, mem_k, mem_v, layer, batch, steps):
    m_tok = mem_k.shape[2]
    return pl.pallas_call(
        _cross_kernel,
        grid=(batch,),
        in_specs=[pl.BlockSpec((None, steps, D_MODEL), lambda b: (b, 0, QCOL_QM)),
                  pl.BlockSpec((None, steps, D_MODEL), lambda b: (b, 0, COL_GC)),
                  pl.BlockSpec((None, None, m_tok, D_MODEL), lambda b: (layer, b, 0, 0)),
                  pl.BlockSpec((None, None, m_tok, D_MODEL), lambda b: (layer, b, 0, 0))],
        out_specs=pl.BlockSpec((None, steps, D_MODEL), lambda b: (b, 0, 0)),
        out_shape=jax.ShapeDtypeStruct((batch, steps, D_MODEL), BF16),
        compiler_params=_params("parallel"),
        name="cross_sample",
    )(pq, pf, mem_k, mem_v)


def _merge_kernel(*refs, alpha, fused_cross):
    if fused_cross:
        oa_ref, ob_ref, qm_ref, gc_ref, mk_ref, mv_ref, *refs = refs
        oc = _cross_attention(qm_ref[...], gc_ref[...], mk_ref, mv_ref)
    else:
        oa_ref, ob_ref, oc_ref, *refs = refs
        oc = oc_ref[...]
    *gm_refs, x_ref, wb_ref, wo_ref, lng_ref, lnb_ref, y_ref = refs
    m = None
    for n, o in enumerate((oa_ref[...], ob_ref[...], oc)):
        gate = jax.nn.sigmoid(gm_refs[n][...])
        term = gate * jnp.dot(o.astype(BF16), wb_ref[n], preferred_element_type=F32)
        m = term if m is None else m + term
    out = jnp.dot(m.astype(BF16), wo_ref[...], preferred_element_type=F32)
    y_ref[...] = _layer_norm_rows(alpha * x_ref[...] + out, lng_ref[...], lnb_ref[...])


def _merge(pq2d, pf2d, o_a, o_b, x, lw, alpha, *, o_c=None, mem_k=None, mem_v=None, seq=None):
    n = x.shape[0]
    fused = o_c is None
    tm = min(n, 512)
    tok = lambda c: pl.BlockSpec((tm, D_MODEL), lambda i: (i, c))
    full = lambda shape: pl.BlockSpec(shape, lambda i: (0,) * len(shape), pipeline_mode=pl.Buffered(1))
    in_specs = [tok(0), tok(0)]
    args = [o_a, o_b]
    if fused:
        assert seq % tm == 0
        per_batch = seq // tm
        m_tok = mem_k.shape[1]
        mem_spec = pl.BlockSpec((None, m_tok, D_MODEL), lambda i: (i // per_batch, 0, 0))
        in_specs += [tok(QCOL_QM), tok(COL_GC), mem_spec, mem_spec]
        args += [pq2d, pf2d, mem_k, mem_v]
    else:
        in_specs += [tok(0)]
        args += [o_c]
    in_specs += [tok(COL_GM + n) for n in range(N_BRANCH)]
    in_specs += [tok(0),
                 full((N_BRANCH, D_MODEL, D_MODEL)), full((D_MODEL, D_MODEL)),
                 full((1, D_MODEL)), full((1, D_MODEL))]
    args += [pf2d] * N_BRANCH + [x, lw["w_branch"], lw["w_o"], lw["ln_g"], lw["ln_b"]]
    kernel = functools.partial(_merge_kernel, alpha=alpha, fused_cross=fused)
    return pl.pallas_call(
        kernel,
        grid=(n // tm,),
        in_specs=in_specs,
        out_specs=pl.BlockSpec((tm, D_MODEL), lambda i: (i, 0)),
        out_shape=jax.ShapeDtypeStruct((n, D_MODEL), F32),
        compiler_params=_params("parallel"),
        name="merge_fused" if fused else "merge",
    )(*args)


def _sections(w, secs):
    return jnp.concatenate([w[:, :, s * D_MODEL:(s + 1) * D_MODEL] for s in secs], axis=-1).astype(BF16)


def kernel(x_prompt, x_sample, cache_k, cache_v, cache_mem_k, cache_mem_v, state_rnn_h, state_conv, mem_prompt,
           ln_in_g, ln_in_b, w_in, lambda_q1, lambda_k1, lambda_q2, lambda_k2, subln_g, conv_w, conv_b, rg_wa,
           rg_ba, rg_wx, rg_bx, rg_lambda, w_mem_kv, w_branch, w_o, ln_g, ln_b):
    bp, sp, _ = x_prompt.shape
    bs, ss, _ = x_sample.shape
    depth = w_in.shape[0]
    past = cache_k.shape[2]
    m_tok = mem_prompt.shape[1]
    alpha = (2 * depth) ** 0.25
    assert sp >= CONV_W - 1 and ss >= CONV_W - 1
    assert w_in.shape[2] == IN_COLS

    w_q_b = _sections(w_in, BF_SECTIONS)
    w_kv_b = _sections(w_in, (SEC_K, SEC_V))
    w_f_b = jnp.concatenate([_sections(w_in, F_SECTIONS), w_in[:, :, SEC_GM * D_MODEL:].astype(BF16)], axis=-1)
    w_mem_b = w_mem_kv.astype(BF16)
    w_branch_b = w_branch.astype(BF16)
    w_o_b = w_o.astype(BF16)
    rg_wa_b = rg_wa.astype(BF16)
    rg_wx_b = rg_wx.astype(BF16)
    cache_k2 = cache_k.reshape(depth, bs, past * A_HEADS, A_V)
    cache_v2 = cache_v.reshape(depth, bs, past * A_HEADS, A_V)
    cmem_k2 = cache_mem_k.reshape(depth, bs, m_tok, D_MODEL)
    cmem_v2 = cache_mem_v.reshape(depth, bs, m_tok, D_MODEL)
    slopes = jnp.exp2(-8.0 * jnp.arange(1, A_HEADS + 1, dtype=F32) / A_HEADS)

    xp = _input_ln(x_prompt.reshape(bp * sp, D_MODEL), ln_in_g, ln_in_b)
    xs = _input_ln(x_sample.reshape(bs * ss, D_MODEL), ln_in_g, ln_in_b)
    mem2d = mem_prompt.reshape(bp * m_tok, D_MODEL)
    h0_p = jnp.zeros((bp, 1, R_W), F32)
    buf0_p = jnp.zeros((bp, CONV_W - 1, R_W), F32)
    nq = len(BF_SECTIONS) * D_MODEL
    nf = (len(F_SECTIONS) + N_BRANCH) * D_MODEL
    k5 = jnp.zeros((depth, bp * sp * A_HEADS, A_V), F32)
    v5 = jnp.zeros((depth, bp * sp * A_HEADS, A_V), F32)

    def sec(p3, col):
        return p3[:, :, col * D_MODEL:(col + 1) * D_MODEL]

    pmk, pmv, ph, pc = [], [], [], []
    sk, sv, sh, sc = [], [], [], []
    for l in range(depth):
        lam_init = 0.8 - 0.6 * math.exp(-0.3 * l)
        lam = (jnp.exp(jnp.sum(lambda_q1[l] * lambda_k1[l])) - jnp.exp(jnp.sum(lambda_q2[l] * lambda_k2[l]))
               + lam_init)
        scal = jnp.concatenate([jnp.stack([lam, jnp.asarray(1.0 - lam_init, F32)]), slopes]).astype(F32)
        sg = subln_g[l].reshape(1, A_V)
        lw = dict(conv_w=conv_w[l], conv_b=conv_b[l].reshape(1, R_W), rg_wa=rg_wa_b[l],
                  rg_ba=rg_ba[l].reshape(1, R_W), rg_wx=rg_wx_b[l], rg_bx=rg_bx[l].reshape(1, R_W),
                  rg_lambda=rg_lambda[l].reshape(1, R_W), w_branch=w_branch_b[l], w_o=w_o_b[l],
                  ln_g=ln_g[l].reshape(1, D_MODEL), ln_b=ln_b[l].reshape(1, D_MODEL))

        mkv = _project(mem2d, w_mem_b, l)
        mk = mkv[:, :D_MODEL].reshape(bp, m_tok, D_MODEL)
        mv = mkv[:, D_MODEL:].reshape(bp, m_tok, D_MODEL)
        pq = _project(xp, w_q_b, l, BF16)
        pf = _project(xp, w_f_b, l)
        kb, vb, k5, v5 = _kv_project(xp, w_kv_b, l, k5, v5)
        pq3 = pq.reshape(bp, sp, nq)
        pf3 = pf.reshape(bp, sp, nf)
        o_a = _attn_prompt(pq3, kb.reshape(bp, sp, D_MODEL), vb.reshape(bp, sp, D_MODEL), pf3, scal, sg, bp, sp)
        o_b, h_new = _rglru(pf3, h0_p, buf0_p, lw, bp, sp)
        xp = _merge(pq, pf, o_a.reshape(bp * sp, D_MODEL), o_b.reshape(bp * sp, D_MODEL), xp, lw, alpha,
                    mem_k=mk, mem_v=mv, seq=sp)
        pmk.append(mk.reshape(bp, m_tok, M_HEADS, M_HD))
        pmv.append(mv.reshape(bp, m_tok, M_HEADS, M_HD))
        ph.append(h_new.reshape(bp, R_W))
        pc.append(sec(pf3, COL_XR)[:, sp - (CONV_W - 1):])

        pq = _project(xs, w_q_b, l, BF16)
        pf = _project(xs, w_f_b, l)
        pkv3 = _project(xs, w_kv_b, l).reshape(bs, ss, 2 * D_MODEL)
        pq3 = pq.reshape(bs, ss, nq)
        pf3 = pf.reshape(bs, ss, nf)
        o_a = _attn_sample(pq3, pkv3, pf3, cache_k2, cache_v2, l, scal, sg, bs, ss)
        o_b, h_new = _rglru(pf3, state_rnn_h[l].reshape(bs, 1, R_W), state_conv[l], lw, bs, ss)
        o_c = _cross_sample(pq3, pf3, cmem_k2, cmem_v2, l, bs, ss)
        xs = _merge(pq, pf, o_a.reshape(bs * ss, D_MODEL), o_b.reshape(bs * ss, D_MODEL), xs, lw, alpha,
                    o_c=o_c.reshape(bs * ss, D_MODEL))
        sk.append(sec(pkv3, KVCOL_K).reshape(bs, ss, A_HEADS, 2 * A_QK))
        sv.append(sec(pkv3, KVCOL_V).reshape(bs, ss, A_HEADS, A_V))
        sh.append(h_new.reshape(bs, R_W))
        sc.append(sec(pf3, COL_XR)[:, ss - (CONV_W - 1):])

    return (xp.reshape(bp, sp, D_MODEL), xs.reshape(bs, ss, D_MODEL),
            k5.reshape(depth, bp, sp, A_HEADS, 2 * A_QK), v5.reshape(depth, bp, sp, A_HEADS, A_V), jnp.stack(pmk), jnp.stack(pmv), jnp.stack(ph), jnp.stack(pc),
            jnp.stack(sk), jnp.stack(sv), jnp.stack(sh), jnp.stack(sc))
```
